```python
import math
import jax
import jax.numpy as jnp
from jax import lax
import numpy as np

D_MODEL = 1024
BATCH = 8
SEQ = 8192
DEPTH = 1

CHUNK = 64
Q_BLOCK = 128

A_HEADS = 8
A_HEAD_DIM = 64
A_WIDTH = A_HEADS * A_HEAD_DIM
IDX_HEADS = 8
IDX_DIM = 64
IDX_TOPK_MAX = 256

B_HEADS = 4
B_QK_DIM = 64
B_V_DIM = 128
B_QK_WIDTH = B_HEADS * B_QK_DIM
B_WIDTH = B_HEADS * B_V_DIM

MIX_WIDTH = A_WIDTH + B_WIDTH
IN_WIDTH = 3 * A_WIDTH + IDX_HEADS * IDX_DIM + IDX_DIM + IDX_HEADS + 2 * B_QK_WIDTH + 2 * B_WIDTH

REL_BUCKETS = 32
REL_MAX_DIST = 128

N_EXPERTS = 32
TOP_K = 4
D_FF = 1024
SWIGLU_LIMIT = 7.0
SWIGLU_ALPHA = 1.702

ROPE_BASE = 10000.0
RMS_EPS = 1e-5
LN_EPS = 1e-6

kernel_name = "hybrid_dsa_retention_moe_block"


def rms_norm(x, gain):
    xf = x.astype(jnp.float32)
    y = xf * lax.rsqrt(jnp.mean(xf * xf, axis=-1, keepdims=True) + RMS_EPS)
    return (y * gain.astype(jnp.float32)).astype(x.dtype)


def layer_norm(x, gain, bias):
    xf = x.astype(jnp.float32)
    mu = jnp.mean(xf, axis=-1, keepdims=True)
    var = jnp.mean(jnp.square(xf - mu), axis=-1, keepdims=True)
    y = (xf - mu) * lax.rsqrt(var + LN_EPS)
    return (y * gain.astype(jnp.float32) + bias.astype(jnp.float32)).astype(x.dtype)


def t5_bucket(rel):
    nb = REL_BUCKETS // 2
    max_exact = nb // 2
    ret = jnp.where(rel > 0, nb, 0)
    n = jnp.abs(rel)
    nf = jnp.maximum(n, 1).astype(jnp.float32)
    large = max_exact + (jnp.log(nf / max_exact) / math.log(REL_MAX_DIST / max_exact)
                         * (nb - max_exact)).astype(jnp.int32)
    large = jnp.minimum(large, nb - 1)
    return ret + jnp.where(n < max_exact, n, large)


def rotary(x, pos):
    half = x.shape[-1] // 2
    inv = 1.0 / (ROPE_BASE ** jnp.linspace(0.0, 1.0, half, dtype=jnp.float32))
    ang = pos.astype(jnp.float32)[:, None] * inv[None, :]
    cos = jnp.cos(ang)[None, :, None, :]
    sin = jnp.sin(ang)[None, :, None, :]
    xf = x.astype(jnp.float32)
    x1, x2 = xf[..., :half], xf[..., half:]
    return jnp.concatenate([x1 * cos - x2 * sin, x2 * cos + x1 * sin], axis=-1).astype(x.dtype)


def dsa_attention(q, k, v, q_idx, k_idx, w_idx, rel_bias, topk):
    bsz, s, h, dh = q.shape
    nb = s // Q_BLOCK
    key_pos = jnp.arange(s, dtype=jnp.int32)
    scale = dh ** -0.5

    def blockify(a):
        return a.reshape(bsz, nb, Q_BLOCK, *a.shape[2:]).swapaxes(0, 1)

    def one_block(args):
        blk, qb, qib, wb = args
        q_pos = blk * Q_BLOCK + jnp.arange(Q_BLOCK, dtype=jnp.int32)
        limit = (q_pos // CHUNK + 1) * CHUNK
        admissible = key_pos[None, :] < limit[:, None]
        dots = jnp.einsum('bqhd,bsd->bqhs', qib, k_idx, preferred_element_type=jnp.float32)
        score = jnp.einsum('bqhs,bqh->bqs', jax.nn.relu(dots), wb.astype(jnp.float32))
        score = jnp.where(admissible[None], score, -jnp.inf)
        _, idx = lax.top_k(score, topk)
        valid = idx < limit[None, :, None]
        k_sel = jax.vmap(lambda kb, ib: kb[ib])(k, idx)
        v_sel = jax.vmap(lambda vb, ib: vb[ib])(v, idx)
        logits = jnp.einsum('bqhd,bqkhd->bhqk', qb, k_sel,
                            preferred_element_type=jnp.float32) * scale
        bias = rel_bias.astype(jnp.float32)[t5_bucket(idx - q_pos[None, :, None])]
        logits = logits + jnp.transpose(bias, (0, 3, 1, 2))
        logits = jnp.where(valid[:, None], logits, -jnp.inf)
        p = jax.nn.softmax(logits, axis=-1).astype(v.dtype)
        return jnp.einsum('bhqk,bqkhd->bqhd', p, v_sel)

    out = lax.map(one_block, (jnp.arange(nb, dtype=jnp.int32), blockify(q),
                              blockify(q_idx), blockify(w_idx)))
    return out.swapaxes(0, 1).reshape(bsz, s, h, dh)


def retention(q, k, v):
    bsz, s, h, dk = q.shape
    dv = v.shape[-1]
    nc = s // CHUNK
    f32 = jnp.float32
    log_g = jnp.log(1.0 - 2.0 ** (-5.0 - jnp.arange(h, dtype=f32)))
    n = jnp.arange(CHUNK, dtype=f32)
    diff = n[:, None] - n[None, :]
    decay = jnp.where(diff >= 0, jnp.exp(log_g[:, None, None] * jnp.maximum(diff, 0.0)), 0.0)
    zeta = jnp.exp(log_g[:, None] * (CHUNK - 1.0 - n)[None, :])
    xi = jnp.exp(log_g[:, None] * (n + 1.0)[None, :])
    chunk_decay = jnp.exp(log_g * CHUNK)
    qc = q.reshape(bsz, nc, CHUNK, h, dk).astype(f32)
    kc = k.reshape(bsz, nc, CHUNK, h, dk).astype(f32) * (dk ** -0.5)
    vc = v.reshape(bsz, nc, CHUNK, h, dv).astype(f32)
    scores = jnp.einsum('bcnhd,bcmhd->bchnm', qc, kc) * decay[None, None]
    intra = jnp.einsum('bchnm,bcmhe->bcnhe', scores, vc)
    kv = jnp.einsum('bcmhd,hm,bcmhe->cbhde', kc, zeta, vc)

    def step(state, kv_c):
        return chunk_decay[None, :, None, None] * state + kv_c, state

    _, prev = lax.scan(step, jnp.zeros((bsz, h, dk, dv), f32), kv)
    cross = jnp.einsum('bcnhd,cbhde,hn->bcnhe', qc, prev, xi)
    return (intra + cross).reshape(bsz, s, h, dv)


def moe_ffn(h, w_router, b_router, w1, b1, w2, b2):
    logits = (jnp.einsum('bsd,de->bse', h, w_router) + b_router).astype(jnp.float32)
    top_val, top_idx = lax.top_k(logits, TOP_K)
    gates = jax.nn.softmax(top_val, axis=-1)
    combine = jnp.sum(jax.nn.one_hot(top_idx, N_EXPERTS, dtype=jnp.float32)
                      * gates[..., None], axis=-2).astype(h.dtype)
    out = jnp.zeros(h.shape, h.dtype)
    for e in range(N_EXPERTS):
        z = jnp.einsum('bsd,df->bsf', h, w1[e]) + b1[e]
        glu = jnp.minimum(z[..., 0::2], SWIGLU_LIMIT)
        lin = jnp.clip(z[..., 1::2], -SWIGLU_LIMIT, SWIGLU_LIMIT)
        act = glu * jax.nn.sigmoid(SWIGLU_ALPHA * glu) * (lin + 1.0)
        out = out + combine[..., e:e + 1] * (jnp.einsum('bsf,fd->bsd', act, w2[e]) + b2[e])
    return out


def setup_inputs(seed: int = 0) -> dict:
    key = jax.random.key(seed)
    ks = jax.random.split(key, 16)
    f32 = jnp.float32

    def nrm(k, shape, scale):
        return jax.random.normal(k, shape, f32) * scale

    return {
        "x": nrm(ks[0], (BATCH, SEQ, D_MODEL), 1.0),
        "norm1_gain": 1.0 + nrm(ks[1], (DEPTH, D_MODEL), 0.02),
        "w_in": nrm(ks[2], (DEPTH, D_MODEL, IN_WIDTH), D_MODEL ** -0.5),
        "rel_bias": nrm(ks[3], (REL_BUCKETS, A_HEADS), 0.5),
        "idx_k_ln_gain": 1.0 + nrm(ks[4], (DEPTH, IDX_DIM), 0.02),
        "idx_k_ln_bias": nrm(ks[5], (DEPTH, IDX_DIM), 0.02),
        "ret_gn_gain": 1.0 + nrm(ks[6], (DEPTH, B_WIDTH), 0.02),
        "w_out": nrm(ks[7], (DEPTH, MIX_WIDTH, D_MODEL), MIX_WIDTH ** -0.5),
        "norm2_gain": 1.0 + nrm(ks[8], (DEPTH, D_MODEL), 0.02),
        "w_router": nrm(ks[9], (DEPTH, D_MODEL, N_EXPERTS), D_MODEL ** -0.5),
        "b_router": nrm(ks[10], (DEPTH, N_EXPERTS), 0.01),
        "w_mlp1": nrm(ks[11], (DEPTH, N_EXPERTS, D_MODEL, 2 * D_FF), D_MODEL ** -0.5),
        "b_mlp1": nrm(ks[12], (DEPTH, N_EXPERTS, 2 * D_FF), 0.01),
        "w_mlp2": nrm(ks[13], (DEPTH, N_EXPERTS, D_FF, D_MODEL), D_FF ** -0.5),
        "b_mlp2": nrm(ks[14], (DEPTH, N_EXPERTS, D_MODEL), 0.01),
        "final_gain": 1.0 + nrm(ks[15], (D_MODEL,), 0.02),
    }


def reference(x, norm1_gain, w_in, rel_bias, idx_k_ln_gain, idx_k_ln_bias, ret_gn_gain,
              w_out, norm2_gain, w_router, b_router, w_mlp1, b_mlp1, w_mlp2, b_mlp2,
              final_gain):
    bsz, s, _ = x.shape
    topk = min(IDX_TOPK_MAX, s // 4)
    pos = jnp.arange(s, dtype=jnp.int32)
    sizes = (A_WIDTH, A_WIDTH, A_WIDTH, IDX_HEADS * IDX_DIM, IDX_DIM, IDX_HEADS,
             B_QK_WIDTH, B_QK_WIDTH, B_WIDTH, B_WIDTH)
    cuts = np.cumsum(sizes)[:-1].tolist()
    idx_w_scale = (IDX_HEADS ** -0.5) * (IDX_DIM ** -0.5)

    for layer in range(DEPTH):
        h = rms_norm(x, norm1_gain[layer])
        proj = jnp.einsum('bsd,dn->bsn', h, w_in[layer])
        q_a, k_a, v_a, q_i, k_i, w_i, q_b, k_b, v_b, g_b = jnp.split(proj, cuts, axis=-1)

        q_a = q_a.reshape(bsz, s, A_HEADS, A_HEAD_DIM)
        k_a = k_a.reshape(bsz, s, A_HEADS, A_HEAD_DIM)
        v_a = v_a.reshape(bsz, s, A_HEADS, A_HEAD_DIM)
        q_i = q_i.reshape(bsz, s, IDX_HEADS, IDX_DIM)
        k_i = layer_norm(k_i, idx_k_ln_gain[layer], idx_k_ln_bias[layer])
        w_i = w_i * idx_w_scale
        out_a = dsa_attention(q_a, k_a, v_a, q_i, k_i, w_i, rel_bias, topk)
        out_a = out_a.reshape(bsz, s, A_WIDTH)

        q_b = rotary(q_b.reshape(bsz, s, B_HEADS, B_QK_DIM), pos)
        k_b = rotary(k_b.reshape(bsz, s, B_HEADS, B_QK_DIM), pos)
        v_b = v_b.reshape(bsz, s, B_HEADS, B_V_DIM)
        y_b = retention(q_b, k_b, v_b)
        mu = jnp.mean(y_b, axis=-1, keepdims=True)
        var = jnp.mean(jnp.square(y_b - mu), axis=-1, keepdims=True)
        y_b = ((y_b - mu) * lax.rsqrt(var + LN_EPS)).reshape(bsz, s, B_WIDTH)
        y_b = (y_b * ret_gn_gain[layer].astype(jnp.float32)).astype(x.dtype)
        out_b = jax.nn.silu(g_b) * y_b

        mixed = jnp.concatenate([out_a, out_b], axis=-1)
        x = x + jnp.einsum('bsm,md->bsd', mixed, w_out[layer])

        h2 = rms_norm(x, norm2_gain[layer])
        x = x + moe_ffn(h2, w_router[layer], b_router[layer], w_mlp1[layer],
                        b_mlp1[layer], w_mlp2[layer], b_mlp2[layer])

    return rms_norm(x, final_gain)
```

```python
import functools
import math

import numpy as np
import jax
import jax.numpy as jnp
from jax import lax
from jax.experimental import pallas as pl
from jax.experimental.pallas import tpu as pltpu

CHUNK = 64
A_HEADS = 8
A_HEAD_DIM = 64
A_WIDTH = A_HEADS * A_HEAD_DIM
IDX_HEADS = 8
IDX_DIM = 64
IDX_TOPK_MAX = 256
B_HEADS = 4
B_QK_DIM = 64
B_V_DIM = 128
B_QK_WIDTH = B_HEADS * B_QK_DIM
B_WIDTH = B_HEADS * B_V_DIM
REL_BUCKETS = 32
REL_MAX_DIST = 128
TOP_K = 4
SWIGLU_LIMIT = 7.0
SWIGLU_ALPHA = 1.702
ROPE_BASE = 10000.0
RMS_EPS = 1e-5
LN_EPS = 1e-6

LANES = 128
VMEM_LIMIT = 56 * 1024 * 1024

MXU_DTYPE = jnp.bfloat16

INT_MIN = -(2 ** 31)
INT_MAX = 2 ** 31 - 1
F32_MIN = float(np.finfo(np.float32).min)

PROJ_TM = 512
RET_C = 256
DSA_T = 256
MOE_TM = 512


def _dot(a, b):
    return jnp.dot(a, b, preferred_element_type=jnp.float32)


def _dot_nt(a, b):
    return lax.dot_general(a, b, (((1,), (1,)), ((), ())), preferred_element_type=jnp.float32)


def _rms(x, gain):
    return x * lax.rsqrt(jnp.mean(x * x, axis=-1, keepdims=True) + RMS_EPS) * gain


_SEG = {}
_off = 0
for _name, _w in (("qa", A_WIDTH), ("ka", A_WIDTH), ("va", A_WIDTH), ("qi", IDX_HEADS * IDX_DIM),
                  ("ki", LANES), ("wi", LANES),
                  ("qb", B_HEADS * LANES), ("qbr", B_HEADS * LANES),
                  ("kb", B_HEADS * LANES), ("kbr", B_HEADS * LANES),
                  ("vb", B_WIDTH), ("gb", B_WIDTH)):
    _SEG[_name] = (_off, _off + _w)
    _off += _w
PROJ_N = _off


def _proj_kernel(x_ref, g_ref, w_ref, cos_ref, sin_ref, lng_ref, lnb_ref,
                 qa_ref, ka_ref, va_ref, qi_ref, ki_ref, wi_ref, qb_ref, kb_ref, vb_ref, sg_ref):
    h = _rms(x_ref[...], g_ref[...]).astype(MXU_DTYPE)

    def seg(name):
        a, b = _SEG[name]
        return _dot(h, w_ref[:, a:b])

    qa_ref[...] = (seg("qa") * (A_HEAD_DIM ** -0.5)).astype(qa_ref.dtype)
    ka_ref[...] = seg("ka").astype(ka_ref.dtype)
    va_ref[...] = seg("va").astype(va_ref.dtype)
    qi_ref[...] = seg("qi").astype(qi_ref.dtype)
    ki = seg("ki")[:, :IDX_DIM]
    mu = jnp.mean(ki, axis=-1, keepdims=True)
    var = jnp.mean(jnp.square(ki - mu), axis=-1, keepdims=True)
    ki = (ki - mu) * lax.rsqrt(var + LN_EPS) * lng_ref[...] + lnb_ref[...]
    ki_ref[...] = ki.astype(ki_ref.dtype)
    wi_ref[...] = seg("wi")[:, :IDX_HEADS] * ((IDX_HEADS ** -0.5) * (IDX_DIM ** -0.5))
    cos = cos_ref[...]
    sin = sin_ref[...]
    qb_ref[...] = (seg("qb") * cos + seg("qbr") * sin).astype(qb_ref.dtype)
    kb_ref[...] = ((seg("kb") * cos + seg("kbr") * sin) * (B_QK_DIM ** -0.5)).astype(kb_ref.dtype)
    vb_ref[...] = seg("vb").astype(vb_ref.dtype)
    g = seg("gb")
    sg_ref[...] = g / (1.0 + jnp.exp(-g))


def _pack_proj_weight(w_in):
    d = w_in.shape[0]
    sizes = (A_WIDTH, A_WIDTH, A_WIDTH, IDX_HEADS * IDX_DIM, IDX_DIM, IDX_HEADS,
             B_QK_WIDTH, B_QK_WIDTH, B_WIDTH, B_WIDTH)
    cuts = np.cumsum(sizes)[:-1].tolist()
    wqa, wka, wva, wqi, wki, wwi, wqb, wkb, wvb, wgb = jnp.split(w_in, cuts, axis=-1)

    def pad_cols(w, n):
        return jnp.pad(w, ((0, 0), (0, n - w.shape[1])))

    def head_slots(w):
        w = w.reshape(d, B_HEADS, B_QK_DIM)
        return jnp.pad(w, ((0, 0), (0, 0), (0, LANES - B_QK_DIM))).reshape(d, B_HEADS * LANES)

    def rot_half(w):
        w = w.reshape(d, B_HEADS, 2, B_QK_DIM // 2)
        return jnp.stack([-w[:, :, 1], w[:, :, 0]], axis=2).reshape(d, B_QK_WIDTH)

    parts = [wqa, wka, wva, wqi, pad_cols(wki, LANES), pad_cols(wwi, LANES),
             head_slots(wqb), head_slots(rot_half(wqb)), head_slots(wkb), head_slots(rot_half(wkb)),
             wvb, wgb]
    return jnp.concatenate(parts, axis=-1).astype(MXU_DTYPE)


def _rotary_tables(s):
    half = B_QK_DIM // 2
    inv = 1.0 / (ROPE_BASE ** jnp.linspace(0.0, 1.0, half, dtype=jnp.float32))
    ang = jnp.arange(s, dtype=jnp.int32).astype(jnp.float32)[:, None] * inv[None, :]

    def slots(t):
        t = jnp.concatenate([t, t, jnp.zeros((s, LANES - B_QK_DIM), jnp.float32)], axis=-1)
        return jnp.tile(t, (1, B_HEADS))

    return slots(jnp.cos(ang)), slots(jnp.sin(ang))


def _in_proj(x2, g1, wp, cos, sin, lng, lnb, s):
    t, d = x2.shape
    tm = min(PROJ_TM, s)
    nt = t // tm
    ns = s // tm
    row = lambda i: (i, 0)
    fixed = lambda i: (0, 0)
    pos = lambda i: (i % ns, 0)
    bw = B_HEADS * LANES
    out_shapes = [
        jax.ShapeDtypeStruct((t, A_WIDTH), MXU_DTYPE),
        jax.ShapeDtypeStruct((t, A_WIDTH), MXU_DTYPE),
        jax.ShapeDtypeStruct((t, A_WIDTH), MXU_DTYPE),
        jax.ShapeDtypeStruct((t, IDX_HEADS * IDX_DIM), MXU_DTYPE),
        jax.ShapeDtypeStruct((t, IDX_DIM), MXU_DTYPE),
        jax.ShapeDtypeStruct((t, IDX_HEADS), jnp.float32),
        jax.ShapeDtypeStruct((t, bw), MXU_DTYPE),
        jax.ShapeDtypeStruct((t, bw), MXU_DTYPE),
        jax.ShapeDtypeStruct((t, B_WIDTH), MXU_DTYPE),
        jax.ShapeDtypeStruct((t, B_WIDTH), jnp.float32),
    ]
    out_specs = [pl.BlockSpec((tm, o.shape[1]), row) for o in out_shapes]
    return pl.pallas_call(
        _proj_kernel,
        grid=(nt,),
        in_specs=[
            pl.BlockSpec((tm, d), row),
            pl.BlockSpec((1, d), fixed),
            pl.BlockSpec((d, PROJ_N), fixed),
            pl.BlockSpec((tm, bw), pos),
            pl.BlockSpec((tm, bw), pos),
            pl.BlockSpec((1, IDX_DIM), fixed),
            pl.BlockSpec((1, IDX_DIM), fixed),
        ],
        out_specs=out_specs,
        out_shape=out_shapes,
        compiler_params=pltpu.CompilerParams(dimension_semantics=("parallel",),
                                             vmem_limit_bytes=VMEM_LIMIT),
        name="in_proj",
    )(x2, g1, wp, cos, sin, lng, lnb)


def _ret_kernel(q_ref, k_ref, v_ref, sg_ref, gain_ref, decay_ref, zeta_ref, xi_ref, cd_ref,
                o_ref, state_ref):
    @pl.when(pl.program_id(1) == 0)
    def _():
        state_ref[...] = jnp.zeros_like(state_ref)

    for h in range(B_HEADS):
        sl = slice(h * LANES, (h + 1) * LANES)
        q = q_ref[0, :, sl]
        k = k_ref[0, :, sl]
        v = v_ref[0, :, sl]
        state = state_ref[h]
        scores = _dot_nt(q, k) * decay_ref[h]
        intra = _dot(scores.astype(MXU_DTYPE), v)
        cross = _dot(q, state.astype(MXU_DTYPE)) * xi_ref[h]
        y = intra + cross
        kz = (k.astype(jnp.float32) * zeta_ref[h]).T.astype(MXU_DTYPE)
        state_ref[h] = cd_ref[h] * state + _dot(kz, v)
        mu = jnp.mean(y, axis=-1, keepdims=True)
        var = jnp.mean(jnp.square(y - mu), axis=-1, keepdims=True)
        yn = (y - mu) * lax.rsqrt(var + LN_EPS) * gain_ref[:, sl]
        o_ref[0, :, sl] = (sg_ref[0, :, sl] * yn).astype(o_ref.dtype)


def _retention(qb, kb, vb, sg, gain, c):
    b, s, bw = qb.shape
    f32 = jnp.float32
    log_g = jnp.log(1.0 - 2.0 ** (-5.0 - jnp.arange(B_HEADS, dtype=f32)))
    n = jnp.arange(c, dtype=f32)
    diff = n[:, None] - n[None, :]
    decay = jnp.where(diff >= 0, jnp.exp(log_g[:, None, None] * jnp.maximum(diff, 0.0)), 0.0)
    zeta = jnp.exp(log_g[:, None] * (c - 1.0 - n)[None, :])[:, :, None]
    xi = jnp.exp(log_g[:, None] * (n + 1.0)[None, :])[:, :, None]
    cd = jnp.broadcast_to(jnp.exp(log_g * c)[:, None, None], (B_HEADS, 1, LANES))
    blk = lambda bi, ci: (bi, ci, 0)
    fix3 = lambda bi, ci: (0, 0, 0)
    return pl.pallas_call(
        _ret_kernel,
        grid=(b, s // c),
        in_specs=[
            pl.BlockSpec((1, c, bw), blk),
            pl.BlockSpec((1, c, bw), blk),
            pl.BlockSpec((1, c, B_WIDTH), blk),
            pl.BlockSpec((1, c, B_WIDTH), blk),
            pl.BlockSpec((1, B_WIDTH), lambda bi, ci: (0, 0)),
            pl.BlockSpec((B_HEADS, c, c), fix3),
            pl.BlockSpec((B_HEADS, c, 1), fix3),
            pl.BlockSpec((B_HEADS, c, 1), fix3),
            pl.BlockSpec((B_HEADS, 1, LANES), fix3),
        ],
        out_specs=pl.BlockSpec((1, c, B_WIDTH), blk),
        out_shape=jax.ShapeDtypeStruct((b, s, B_WIDTH), MXU_DTYPE),
        scratch_shapes=[pltpu.VMEM((B_HEADS, LANES, LANES), jnp.float32)],
        compiler_params=pltpu.CompilerParams(dimension_semantics=("parallel", "arbitrary"),
                                             vmem_limit_bytes=VMEM_LIMIT),
        name="retention",
    )(qb, kb, vb, sg, gain, decay, zeta, xi, cd)


def _t5_bucket(rel):
    nb = REL_BUCKETS // 2
    max_exact = nb // 2
    ret = jnp.where(rel > 0, nb, 0)
    n = jnp.abs(rel)
    nf = jnp.maximum(n, 1).astype(jnp.float32)
    large = max_exact + (jnp.log(nf / max_exact) / math.log(REL_MAX_DIST / max_exact)
                         * (nb - max_exact)).astype(jnp.int32)
    large = jnp.minimum(large, nb - 1)
    return ret + jnp.where(n < max_exact, n, large)


def _bias_tiles(rel_bias, t):
    i = jnp.arange(t, dtype=jnp.int32)
    rel_d = i[None, :] - i[:, None]
    rel_p = rel_d - t
    rb = rel_bias.astype(jnp.float32)
    far = rb[_t5_bucket(jnp.int32(-(t + 1)))]
    bd = jnp.transpose(rb[_t5_bucket(rel_d)], (2, 0, 1)) - far[:, None, None]
    bp = jnp.transpose(rb[_t5_bucket(rel_p)], (2, 0, 1)) - far[:, None, None]
    return bd, bp


def _dsa_kernel(qa_ref, qi_ref, wi_ref, kt_ref, v_ref, kit_ref, bd_ref, bp_ref, o_ref,
                keys_ref, m_ref, l_ref, acc_ref, *, topk):
    t = DSA_T
    qblk = pl.program_id(1)
    ntile = qblk + 1
    i32 = jnp.int32
    row = lax.broadcasted_iota(i32, (t, 1), 0)
    lim = qblk * t + (row // CHUNK + 1) * CHUNK
    lane_t = lax.broadcasted_iota(i32, (t, t), 1)

    def tile_off(j):
        return pl.multiple_of(j * t, t)

    qi = qi_ref[0]
    wi = wi_ref[0]
    qi_h = [qi[:, h * IDX_DIM:(h + 1) * IDX_DIM] for h in range(IDX_HEADS)]
    wi_h = [wi[:, h:h + 1] for h in range(IDX_HEADS)]

    def score_tile(j, diag):
        off = tile_off(j)
        kt = kit_ref[0, :, pl.ds(off, t)]
        sc = jnp.zeros((t, t), jnp.float32)
        for h in range(IDX_HEADS):
            sc = sc + wi_h[h] * jnp.maximum(_dot(qi_h[h], kt), 0.0)
        sc = jnp.where(sc == 0.0, 0.0, sc)
        bits = pltpu.bitcast(sc, i32)
        key = bits ^ ((bits >> 31) & INT_MAX)
        if diag:
            key = jnp.where(lane_t < (lim - qblk * t), key, INT_MIN)
        keys_ref[:, pl.ds(off, t)] = key

    def far_scores(j, c):
        score_tile(j, False)
        return c

    lax.fori_loop(0, qblk, far_scores, 0)
    score_tile(qblk, True)

    def count(pred):
        def body(j, cnt):
            off = tile_off(j)
            hit = jnp.where(pred(keys_ref[:, pl.ds(off, t)], lane_t + j * t), 1, 0)
            for c0 in range(0, t, LANES):
                cnt = cnt + hit[:, c0:c0 + LANES]
            return cnt
        cnt = lax.fori_loop(0, ntile, body, jnp.zeros((t, LANES), i32))
        return jnp.sum(cnt, axis=1, keepdims=True)

    def bisect(_, carry):
        lo, hi = carry
        mid = (lo >> 1) + (hi >> 1) + (lo & hi & 1)
        ge = count(lambda k, col: k >= mid) >= topk
        return jnp.where(ge, mid, lo), jnp.where(ge, hi, mid)

    lo, _ = lax.fori_loop(0, 32, bisect,
                          (jnp.full((t, 1), INT_MIN, i32), jnp.full((t, 1), INT_MAX, i32)))
    keep_all = lim <= topk
    tau = jnp.where(keep_all, INT_MIN + 1, lo)
    n_ge = count(lambda k, col: k >= tau)
    tie = jnp.logical_and(n_ge > topk, jnp.logical_not(keep_all))

    @pl.when(jnp.sum(jnp.where(tie, 1, 0)) > 0)
    def _():
        n_eq_keep = topk - count(lambda k, col: k > tau)

        def bis_pos(_, carry):
            plo, phi = carry
            pmid = (plo + phi) >> 1
            ok = count(lambda k, col: jnp.logical_and(k == tau, col < pmid)) <= n_eq_keep
            return jnp.where(ok, pmid, plo), jnp.where(ok, phi, pmid)

        smax = pl.num_programs(1) * t
        nbits = int(math.ceil(math.log2(keys_ref.shape[1]))) + 1
        plo, _ = lax.fori_loop(0, nbits, bis_pos,
                               (jnp.zeros((t, 1), i32), jnp.zeros((t, 1), i32) + smax))
        cut = jnp.where(tie, plo, smax)

        def drop(j, c):
            off = tile_off(j)
            k = keys_ref[:, pl.ds(off, t)]
            dead = jnp.logical_and(k == tau, (lane_t + j * t) >= cut)
            keys_ref[:, pl.ds(off, t)] = jnp.where(dead, INT_MIN, k)
            return c

        lax.fori_loop(0, ntile, drop, 0)

    m_ref[...] = jnp.full(m_ref.shape, F32_MIN, jnp.float32)
    l_ref[...] = jnp.zeros(l_ref.shape, jnp.float32)
    acc_ref[...] = jnp.zeros(acc_ref.shape, jnp.float32)
    qa = qa_ref[0]
    lane_p = lax.broadcasted_iota(i32, (t, LANES), 1)
    qm = []
    for h in range(A_HEADS):
        pair = qa[:, (h // 2) * LANES:(h // 2 + 1) * LANES]
        mine = (lane_p // A_HEAD_DIM) == (h % 2)
        qm.append(jnp.where(mine, pair, jnp.zeros_like(pair)))

    def attn_tile(j, bias_ref):
        off = tile_off(j)
        sel = keys_ref[:, pl.ds(off, t)] >= tau
        for h in range(A_HEADS):
            g = h // 2
            s = _dot(qm[h], kt_ref[0, g * LANES:(g + 1) * LANES, pl.ds(off, t)])
            if bias_ref is not None:
                s = s + bias_ref[h]
            s = jnp.where(sel, s, -jnp.inf)
            m_old = m_ref[h]
            m_new = jnp.maximum(m_old, jnp.max(s, axis=1, keepdims=True))
            alpha = jnp.exp(m_old - m_new)
            p = jnp.exp(s - m_new)
            l_ref[h] = alpha * l_ref[h] + jnp.sum(p, axis=1, keepdims=True)
            pv = _dot(p.astype(MXU_DTYPE), v_ref[0, pl.ds(off, t), g * LANES:(g + 1) * LANES])
            acc_ref[h] = alpha * acc_ref[h] + pv
            m_ref[h] = m_new

    def far_attn(j, c):
        attn_tile(j, None)
        return c

    lax.fori_loop(0, qblk - 1, far_attn, 0)

    @pl.when(qblk >= 1)
    def _():
        attn_tile(qblk - 1, bp_ref)

    attn_tile(qblk, bd_ref)

    for g in range(A_HEADS // 2):
        even = acc_ref[2 * g] / l_ref[2 * g]
        odd = acc_ref[2 * g + 1] / l_ref[2 * g + 1]
        o_ref[0, :, g * LANES:(g + 1) * LANES] = jnp.where(
            lane_p < A_HEAD_DIM, even, odd).astype(o_ref.dtype)


def _dsa(qa, qi, wi, kt, v, kit, bd, bp, topk):
    b, s, _ = qa.shape
    t = DSA_T
    qblk = lambda bi, qi_: (bi, qi_, 0)
    per_b = lambda bi, qi_: (bi, 0, 0)
    fix3 = lambda bi, qi_: (0, 0, 0)
    once = pl.Buffered(1)
    return pl.pallas_call(
        functools.partial(_dsa_kernel, topk=topk),
        grid=(b, s // t),
        in_specs=[
            pl.BlockSpec((1, t, A_WIDTH), qblk),
            pl.BlockSpec((1, t, IDX_HEADS * IDX_DIM), qblk),
            pl.BlockSpec((1, t, IDX_HEADS), qblk),
            pl.BlockSpec((1, A_WIDTH, s), per_b, pipeline_mode=once),
            pl.BlockSpec((1, s, A_WIDTH), per_b, pipeline_mode=once),
            pl.BlockSpec((1, IDX_DIM, s), per_b, pipeline_mode=once),
            pl.BlockSpec((A_HEADS, t, t), fix3, pipeline_mode=once),
            pl.BlockSpec((A_HEADS, t, t), fix3, pipeline_mode=once),
        ],
        out_specs=pl.BlockSpec((1, t, A_WIDTH), qblk),
        out_shape=jax.ShapeDtypeStruct((b, s, A_WIDTH), MXU_DTYPE),
        scratch_shapes=[
            pltpu.VMEM((t, s), jnp.int32),
            pltpu.VMEM((A_HEADS, t, 1), jnp.float32),
            pltpu.VMEM((A_HEADS, t, 1), jnp.float32),
            pltpu.VMEM((A_HEADS, t, LANES), jnp.float32),
        ],
        compiler_params=pltpu.CompilerParams(dimension_semantics=("parallel", "arbitrary"),
                                             vmem_limit_bytes=VMEM_LIMIT),
        name="dsa",
    )(qa, qi, wi, kt, v, kit, bd, bp)


def _outproj_kernel(x_ref, oa_ref, ob_ref, wa_ref, wb_ref, g2_ref, wr_ref, br_ref,
                    x1_ref, h2_ref, comb_ref):
    x1 = x_ref[...] + _dot(oa_ref[...], wa_ref[...]) + _dot(ob_ref[...], wb_ref[...])
    x1_ref[...] = x1
    h2 = _rms(x1, g2_ref[...])
    h2m = h2.astype(MXU_DTYPE)
    h2_ref[...] = h2m
    logits = _dot(h2m, wr_ref[...]) + br_ref[...]
    lane = lax.broadcasted_iota(jnp.int32, logits.shape, 1)
    work = logits
    vals, hots = [], []
    for _ in range(TOP_K):
        mx = jnp.max(work, axis=1, keepdims=True)
        idx = jnp.min(jnp.where(work == mx, lane, LANES), axis=1, keepdims=True)
        hot = lane == idx
        vals.append(mx)
        hots.append(hot)
        work = jnp.where(hot, -jnp.inf, work)
    es = [jnp.exp(v - vals[0]) for v in vals]
    denom = es[0]
    for e in es[1:]:
        denom = denom + e
    comb = jnp.zeros(logits.shape, jnp.float32)
    for e, hot in zip(es, hots):
        comb = comb + jnp.where(hot, e / denom, 0.0)
    comb_ref[...] = comb


def _out_proj(x2, oa, ob, wa, wb, g2, wr, br):
    t, d = x2.shape
    tm = min(PROJ_TM, t)
    row = lambda i: (i, 0)
    fixed = lambda i: (0, 0)
    return pl.pallas_call(
        _outproj_kernel,
        grid=(t // tm,),
        in_specs=[
            pl.BlockSpec((tm, d), row),
            pl.BlockSpec((tm, A_WIDTH), row),
            pl.BlockSpec((tm, B_WIDTH), row),
            pl.BlockSpec((A_WIDTH, d), fixed),
            pl.BlockSpec((B_WIDTH, d), fixed),
            pl.BlockSpec((1, d), fixed),
            pl.BlockSpec((d, LANES), fixed),
            pl.BlockSpec((1, LANES), fixed),
        ],
        out_specs=[pl.BlockSpec((tm, d), row), pl.BlockSpec((tm, d), row), pl.BlockSpec((tm, LANES), row)],
        out_shape=[jax.ShapeDtypeStruct((t, d), jnp.float32),
                   jax.ShapeDtypeStruct((t, d), MXU_DTYPE),
                   jax.ShapeDtypeStruct((t, LANES), jnp.float32)],
        compiler_params=pltpu.CompilerParams(dimension_semantics=("parallel",),
                                             vmem_limit_bytes=VMEM_LIMIT),
        name="out_proj",
    )(x2, oa, ob, wa, wb, g2, wr, br)


def _moe_kernel(x1_ref, h2_ref, comb_ref, w1_ref, b1_ref, w2_ref, b2_ref, gf_ref, o_ref, acc_ref):
    e = pl.program_id(1)

    @pl.when(e == 0)
    def _():
        acc_ref[...] = jnp.zeros_like(acc_ref)

    comb = comb_ref[...]
    lane = lax.broadcasted_iota(jnp.int32, comb.shape, 1)
    c = jnp.sum(jnp.where(lane == e, comb, 0.0), axis=1, keepdims=True)
    ff = w2_ref.shape[1]
    z = _dot(h2_ref[...], w1_ref[0]) + b1_ref[0]
    glu = jnp.minimum(z[:, :ff], SWIGLU_LIMIT)
    lin = jnp.clip(z[:, ff:], -SWIGLU_LIMIT, SWIGLU_LIMIT)
    act = glu * (1.0 / (1.0 + jnp.exp(-SWIGLU_ALPHA * glu))) * (lin + 1.0)
    y = _dot(act.astype(MXU_DTYPE), w2_ref[0]) + b2_ref[0]
    acc_ref[...] += c * y

    @pl.when(e == pl.num_programs(1) - 1)
    def _():
        o_ref[...] = _rms(x1_ref[...] + acc_ref[...], gf_ref[...])


def _moe(x1, h2, comb, w1, b1, w2, b2, gf):
    t, d = x1.shape
    ne, _, f2 = w1.shape
    ff = f2 // 2
    tm = min(MOE_TM, t)
    row = lambda i, e: (i, 0)
    return pl.pallas_call(
        _moe_kernel,
        grid=(t // tm, ne),
        in_specs=[
            pl.BlockSpec((tm, d), row),
            pl.BlockSpec((tm, d), row),
            pl.BlockSpec((tm, LANES), row),
            pl.BlockSpec((1, d, f2), lambda i, e: (e, 0, 0)),
            pl.BlockSpec((1, 1, f2), lambda i, e: (e, 0, 0)),
            pl.BlockSpec((1, ff, d), lambda i, e: (e, 0, 0)),
            pl.BlockSpec((1, 1, d), lambda i, e: (e, 0, 0)),
            pl.BlockSpec((1, d), lambda i, e: (0, 0)),
        ],
        out_specs=pl.BlockSpec((tm, d), row),
        out_shape=jax.ShapeDtypeStruct((t, d), jnp.float32),
        scratch_shapes=[pltpu.VMEM((tm, d), jnp.float32)],
        compiler_params=pltpu.CompilerParams(dimension_semantics=("parallel", "arbitrary"),
                                             vmem_limit_bytes=VMEM_LIMIT),
        name="moe",
    )(x1, h2, comb, w1, b1, w2, b2, gf)


def kernel(x, norm1_gain, w_in, rel_bias, idx_k_ln_gain, idx_k_ln_bias, ret_gn_gain,
           w_out, norm2_gain, w_router, b_router, w_mlp1, b_mlp1, w_mlp2, b_mlp2, final_gain):
    bsz, s, d = x.shape
    depth = w_in.shape[0]
    topk = min(IDX_TOPK_MAX, s // 4)
    assert depth == 1, "the final RMSNorm is fused into the expert kernel of the single layer"
    assert s % DSA_T == 0 and s % RET_C == 0 and topk <= DSA_T and DSA_T % CHUNK == 0
    ne = w_router.shape[-1]
    assert ne <= LANES
    f32 = jnp.float32
    x2 = x.reshape(bsz * s, d).astype(f32)
    cos, sin = _rotary_tables(s)
    bd, bp = _bias_tiles(rel_bias, DSA_T)

    for layer in range(depth):
        wp = _pack_proj_weight(w_in[layer])
        qa, ka, va, qi, ki, wi, qb, kb, vb, sg = _in_proj(
            x2, norm1_gain[layer][None, :].astype(f32), wp, cos, sin,
            idx_k_ln_gain[layer][None, :].astype(f32), idx_k_ln_bias[layer][None, :].astype(f32), s)

        r3 = lambda a: a.reshape(bsz, s, a.shape[-1])
        out_b = _retention(r3(qb), r3(kb), r3(vb), r3(sg),
                           ret_gn_gain[layer][None, :].astype(f32), min(RET_C, s))
        kt = jnp.swapaxes(r3(ka), 1, 2)
        kit = jnp.swapaxes(r3(ki), 1, 2)
        out_a = _dsa(r3(qa), r3(qi), r3(wi), kt, r3(va), kit, bd, bp, topk)

        wo = w_out[layer].astype(MXU_DTYPE)
        wr = jnp.pad(w_router[layer], ((0, 0), (0, LANES - ne))).astype(MXU_DTYPE)
        br = jnp.pad(b_router[layer].astype(f32), (0, LANES - ne), constant_values=-jnp.inf)[None, :]
        x1, h2, comb = _out_proj(x2, out_a.reshape(bsz * s, A_WIDTH), out_b.reshape(bsz * s, B_WIDTH),
                                 wo[:A_WIDTH], wo[A_WIDTH:], norm2_gain[layer][None, :].astype(f32), wr, br)

        w1 = w_mlp1[layer]
        w1p = jnp.concatenate([w1[..., 0::2], w1[..., 1::2]], axis=-1).astype(MXU_DTYPE)
        b1 = b_mlp1[layer].astype(f32)
        b1p = jnp.concatenate([b1[..., 0::2], b1[..., 1::2]], axis=-1)[:, None, :]
        x2 = _moe(x1, h2, comb, w1p, b1p, w_mlp2[layer].astype(MXU_DTYPE),
                  b_mlp2[layer].astype(f32)[:, None, :], final_gain[None, :].astype(f32))

    return x2.reshape(bsz, s, d).astype(x.dtype)
```

```python
import functools
import math

import numpy as np
import jax
import jax.numpy as jnp
from jax import lax
from jax.experimental import pallas as pl
from jax.experimental.pallas import tpu as pltpu

CHUNK = 64
A_HEADS = 8
A_HEAD_DIM = 64
A_WIDTH = A_HEADS * A_HEAD_DIM
IDX_HEADS = 8
IDX_DIM = 64
IDX_TOPK_MAX = 256
B_HEADS = 4
B_QK_DIM = 64
B_V_DIM = 128
B_QK_WIDTH = B_HEADS * B_QK_DIM
B_WIDTH = B_HEADS * B_V_DIM
REL_BUCKETS = 32
REL_MAX_DIST = 128
TOP_K = 4
SWIGLU_LIMIT = 7.0
SWIGLU_ALPHA = 1.702
ROPE_BASE = 10000.0
RMS_EPS = 1e-5
LN_EPS = 1e-6

LANES = 128
SUBLANES = 8
VMEM_LIMIT = 56 * 1024 * 1024

MXU_DTYPE = jnp.bfloat16

INT_MIN = -(2 ** 31)
INT_MAX = 2 ** 31 - 1
F32_MIN = float(np.finfo(np.float32).min)

PROJ_TM = 512
RET_C = 256
DSA_T = 256
MOE_TM = 512


def _dot(a, b):
    return jnp.dot(a, b, preferred_element_type=jnp.float32)


def _dot_nt(a, b):
    return lax.dot_general(a, b, (((1,), (1,)), ((), ())), preferred_element_type=jnp.float32)


def _rms(x, gain):
    return x * lax.rsqrt(jnp.mean(x * x, axis=-1, keepdims=True) + RMS_EPS) * gain


_SEG = {}
_off = 0
for _name, _w in (("qa", A_WIDTH), ("ka", A_WIDTH), ("va", A_WIDTH), ("qi", IDX_HEADS * IDX_DIM),
                  ("ki", LANES), ("wi", LANES),
                  ("qb", B_HEADS * LANES), ("qbr", B_HEADS * LANES),
                  ("kb", B_HEADS * LANES), ("kbr", B_HEADS * LANES),
                  ("vb", B_WIDTH), ("gb", B_WIDTH)):
    _SEG[_name] = (_off, _off + _w)
    _off += _w
PROJ_N = _off


def _proj_kernel(x_ref, g_ref, w_ref, cos_ref, sin_ref, lng_ref, lnb_ref,
                 qa_ref, ka_ref, va_ref, qi_ref, ki_ref, wi_ref, qb_ref, kb_ref, vb_ref, sg_ref):
    h = _rms(x_ref[...], g_ref[...]).astype(MXU_DTYPE)

    def seg(name):
        a, b = _SEG[name]
        return _dot(h, w_ref[:, a:b])

    qa_ref[...] = (seg("qa") * (A_HEAD_DIM ** -0.5)).astype(qa_ref.dtype)
    ka_ref[...] = seg("ka").astype(ka_ref.dtype)
    va_ref[...] = seg("va").astype(va_ref.dtype)
    qi_ref[...] = seg("qi").astype(qi_ref.dtype)
    ki = seg("ki")[:, :IDX_DIM]
    mu = jnp.mean(ki, axis=-1, keepdims=True)
    var = jnp.mean(jnp.square(ki - mu), axis=-1, keepdims=True)
    ki = (ki - mu) * lax.rsqrt(var + LN_EPS) * lng_ref[...] + lnb_ref[...]
    ki_ref[...] = ki.astype(ki_ref.dtype)
    wi_ref[...] = seg("wi")[:, :IDX_HEADS] * ((IDX_HEADS ** -0.5) * (IDX_DIM ** -0.5))
    cos = cos_ref[...]
    sin = sin_ref[...]
    qb_ref[...] = (seg("qb") * cos + seg("qbr") * sin).astype(qb_ref.dtype)
    kb_ref[...] = ((seg("kb") * cos + seg("kbr") * sin) * (B_QK_DIM ** -0.5)).astype(kb_ref.dtype)
    vb_ref[...] = seg("vb").astype(vb_ref.dtype)
    g = seg("gb")
    sg_ref[...] = g / (1.0 + jnp.exp(-g))


def _pack_proj_weight(w_in):
    d = w_in.shape[0]
    sizes = (A_WIDTH, A_WIDTH, A_WIDTH, IDX_HEADS * IDX_DIM, IDX_DIM, IDX_HEADS,
             B_QK_WIDTH, B_QK_WIDTH, B_WIDTH, B_WIDTH)
    cuts = np.cumsum(sizes)[:-1].tolist()
    wqa, wka, wva, wqi, wki, wwi, wqb, wkb, wvb, wgb = jnp.split(w_in, cuts, axis=-1)

    def pad_cols(w, n):
        return jnp.pad(w, ((0, 0), (0, n - w.shape[1])))

    def head_slots(w):
        w = w.reshape(d, B_HEADS, B_QK_DIM)
        return jnp.pad(w, ((0, 0), (0, 0), (0, LANES - B_QK_DIM))).reshape(d, B_HEADS * LANES)

    def rot_half(w):
        w = w.reshape(d, B_HEADS, 2, B_QK_DIM // 2)
        return jnp.stack([-w[:, :, 1], w[:, :, 0]], axis=2).reshape(d, B_QK_WIDTH)

    parts = [wqa, wka, wva, wqi, pad_cols(wki, LANES), pad_cols(wwi, LANES),
             head_slots(wqb), head_slots(rot_half(wqb)), head_slots(wkb), head_slots(rot_half(wkb)),
             wvb, wgb]
    return jnp.concatenate(parts, axis=-1).astype(MXU_DTYPE)


def _rotary_tables(s):
    half = B_QK_DIM // 2
    inv = 1.0 / (ROPE_BASE ** jnp.linspace(0.0, 1.0, half, dtype=jnp.float32))
    ang = jnp.arange(s, dtype=jnp.int32).astype(jnp.float32)[:, None] * inv[None, :]

    def slots(t):
        t = jnp.concatenate([t, t, jnp.zeros((s, LANES - B_QK_DIM), jnp.float32)], axis=-1)
        return jnp.tile(t, (1, B_HEADS))

    return slots(jnp.cos(ang)), slots(jnp.sin(ang))


def _in_proj(x2, g1, wp, cos, sin, lng, lnb, s):
    t, d = x2.shape
    tm = min(PROJ_TM, s)
    nt = t // tm
    ns = s // tm
    row = lambda i: (i, 0)
    fixed = lambda i: (0, 0)
    pos = lambda i: (i % ns, 0)
    bw = B_HEADS * LANES
    out_shapes = [
        jax.ShapeDtypeStruct((t, A_WIDTH), MXU_DTYPE),
        jax.ShapeDtypeStruct((t, A_WIDTH), MXU_DTYPE),
        jax.ShapeDtypeStruct((t, A_WIDTH), MXU_DTYPE),
        jax.ShapeDtypeStruct((t, IDX_HEADS * IDX_DIM), MXU_DTYPE),
        jax.ShapeDtypeStruct((t, IDX_DIM), MXU_DTYPE),
        jax.ShapeDtypeStruct((t, IDX_HEADS), jnp.float32),
        jax.ShapeDtypeStruct((t, bw), MXU_DTYPE),
        jax.ShapeDtypeStruct((t, bw), MXU_DTYPE),
        jax.ShapeDtypeStruct((t, B_WIDTH), MXU_DTYPE),
        jax.ShapeDtypeStruct((t, B_WIDTH), jnp.float32),
    ]
    out_specs = [pl.BlockSpec((tm, o.shape[1]), row) for o in out_shapes]
    return pl.pallas_call(
        _proj_kernel,
        grid=(nt,),
        in_specs=[
            pl.BlockSpec((tm, d), row),
            pl.BlockSpec((1, d), fixed),
            pl.BlockSpec((d, PROJ_N), fixed),
            pl.BlockSpec((tm, bw), pos),
            pl.BlockSpec((tm, bw), pos),
            pl.BlockSpec((1, IDX_DIM), fixed),
            pl.BlockSpec((1, IDX_DIM), fixed),
        ],
        out_specs=out_specs,
        out_shape=out_shapes,
        compiler_params=pltpu.CompilerParams(dimension_semantics=("parallel",),
                                             vmem_limit_bytes=VMEM_LIMIT),
        name="in_proj",
    )(x2, g1, wp, cos, sin, lng, lnb)


def _ret_kernel(q_ref, k_ref, v_ref, sg_ref, gain_ref, decay_ref, zeta_ref, xi_ref, cd_ref,
                o_ref, state_ref):
    @pl.when(pl.program_id(1) == 0)
    def _():
        state_ref[...] = jnp.zeros_like(state_ref)

    for h in range(B_HEADS):
        sl = slice(h * LANES, (h + 1) * LANES)
        q = q_ref[0, :, sl]
        k = k_ref[0, :, sl]
        v = v_ref[0, :, sl]
        state = state_ref[h]
        scores = _dot_nt(q, k) * decay_ref[h]
        intra = _dot(scores.astype(MXU_DTYPE), v)
        cross = _dot(q, state.astype(MXU_DTYPE)) * xi_ref[h]
        y = intra + cross
        kz = (k.astype(jnp.float32) * zeta_ref[h]).T.astype(MXU_DTYPE)
        state_ref[h] = cd_ref[h] * state + _dot(kz, v)
        mu = jnp.mean(y, axis=-1, keepdims=True)
        var = jnp.mean(jnp.square(y - mu), axis=-1, keepdims=True)
        yn = (y - mu) * lax.rsqrt(var + LN_EPS) * gain_ref[:, sl]
        o_ref[0, :, sl] = (sg_ref[0, :, sl] * yn).astype(o_ref.dtype)


def _retention(qb, kb, vb, sg, gain, c):
    b, s, bw = qb.shape
    f32 = jnp.float32
    log_g = jnp.log(1.0 - 2.0 ** (-5.0 - jnp.arange(B_HEADS, dtype=f32)))
    n = jnp.arange(c, dtype=f32)
    diff = n[:, None] - n[None, :]
    decay = jnp.where(diff >= 0, jnp.exp(log_g[:, None, None] * jnp.maximum(diff, 0.0)), 0.0)
    zeta = jnp.exp(log_g[:, None] * (c - 1.0 - n)[None, :])[:, :, None]
    xi = jnp.exp(log_g[:, None] * (n + 1.0)[None, :])[:, :, None]
    cd = jnp.broadcast_to(jnp.exp(log_g * c)[:, None, None], (B_HEADS, 1, LANES))
    blk = lambda bi, ci: (bi, ci, 0)
    fix3 = lambda bi, ci: (0, 0, 0)
    return pl.pallas_call(
        _ret_kernel,
        grid=(b, s // c),
        in_specs=[
            pl.BlockSpec((1, c, bw), blk),
            pl.BlockSpec((1, c, bw), blk),
            pl.BlockSpec((1, c, B_WIDTH), blk),
            pl.BlockSpec((1, c, B_WIDTH), blk),
            pl.BlockSpec((1, B_WIDTH), lambda bi, ci: (0, 0)),
            pl.BlockSpec((B_HEADS, c, c), fix3),
            pl.BlockSpec((B_HEADS, c, 1), fix3),
            pl.BlockSpec((B_HEADS, c, 1), fix3),
            pl.BlockSpec((B_HEADS, 1, LANES), fix3),
        ],
        out_specs=pl.BlockSpec((1, c, B_WIDTH), blk),
        out_shape=jax.ShapeDtypeStruct((b, s, B_WIDTH), MXU_DTYPE),
        scratch_shapes=[pltpu.VMEM((B_HEADS, LANES, LANES), jnp.float32)],
        compiler_params=pltpu.CompilerParams(dimension_semantics=("parallel", "arbitrary"),
                                             vmem_limit_bytes=VMEM_LIMIT),
        name="retention",
    )(qb, kb, vb, sg, gain, decay, zeta, xi, cd)


def _t5_bucket(rel):
    nb = REL_BUCKETS // 2
    max_exact = nb // 2
    ret = jnp.where(rel > 0, nb, 0)
    n = jnp.abs(rel)
    nf = jnp.maximum(n, 1).astype(jnp.float32)
    large = max_exact + (jnp.log(nf / max_exact) / math.log(REL_MAX_DIST / max_exact)
                         * (nb - max_exact)).astype(jnp.int32)
    large = jnp.minimum(large, nb - 1)
    return ret + jnp.where(n < max_exact, n, large)


def _bias_tiles(rel_bias, t):
    i = jnp.arange(t, dtype=jnp.int32)
    rel_d = i[:, None] - i[None, :]
    rb = rel_bias.astype(jnp.float32)
    far = rb[_t5_bucket(jnp.int32(-(t + 1)))]

    def tile(rel):
        hot = jax.nn.one_hot(_t5_bucket(rel), REL_BUCKETS, dtype=jnp.float32)
        return jnp.einsum("kqb,bh->hkq", hot, rb, precision=lax.Precision.HIGHEST) - far[:, None, None]

    return tile(rel_d), tile(rel_d - t)


def _dsa_kernel(qat_ref, qit_ref, wit_ref, k_ref, vt_ref, ki_ref, bd_ref, bp_ref, o_ref,
                keys_ref, m_ref, l_ref, acc_ref, *, topk):
    t = DSA_T
    i32 = jnp.int32
    qblk = pl.program_id(1)
    ntile = qblk + 1
    qcol = lax.broadcasted_iota(i32, (1, t), 1)
    lim_local = (qcol // CHUNK + 1) * CHUNK
    keep_all = (qblk * t + lim_local) <= topk
    krow = lax.broadcasted_iota(i32, (t, t), 0)

    def tile_off(j):
        return pl.multiple_of(j * t, t)

    qit = [qit_ref[0, h * IDX_DIM:(h + 1) * IDX_DIM, :] for h in range(IDX_HEADS)]
    wit = [wit_ref[0, h:h + 1, :] for h in range(IDX_HEADS)]

    def score_tile(j, diag):
        off = tile_off(j)
        kt = ki_ref[0, pl.ds(off, t), :]
        sc = jnp.zeros((t, t), jnp.float32)
        for h in range(IDX_HEADS):
            sc = sc + wit[h] * jnp.maximum(_dot(kt, qit[h]), 0.0)
        sc = jnp.where(sc == 0.0, 0.0, sc)
        bits = pltpu.bitcast(sc, i32)
        key = bits ^ ((bits >> 31) & INT_MAX)
        if diag:
            key = jnp.where(krow < lim_local, key, INT_MIN)
        keys_ref[pl.ds(off, t), :] = key

    def far_scores(j, c):
        score_tile(j, False)
        return c

    lax.fori_loop(0, qblk, far_scores, 0)
    score_tile(qblk, True)

    sub = lax.broadcasted_iota(i32, (SUBLANES, t), 0)

    def rows8(v):
        return jnp.broadcast_to(v, (SUBLANES, t))

    def count(pred):
        def body(j, cnt):
            off = tile_off(j)
            for r in range(0, t, SUBLANES):
                start = pl.multiple_of(off + r, SUBLANES)
                cnt = cnt + jnp.where(pred(keys_ref[pl.ds(start, SUBLANES), :], sub + start), 1, 0)
            return cnt
        cnt = lax.fori_loop(0, ntile, body, jnp.zeros((SUBLANES, t), i32))
        return jnp.sum(cnt, axis=0, keepdims=True)

    def unsettled(lo, hi):
        return jnp.max(jnp.where(hi != lo + 1, 1, 0))

    def bis_cond(c):
        it, active, _, _ = c
        return jnp.logical_and(it < 32, active > 0)

    def bis_body(c):
        it, _, lo, hi = c
        mid = (lo >> 1) + (hi >> 1) + (lo & hi & 1)
        mid8 = rows8(mid)
        n = count(lambda k, idx: k >= mid8)
        ge = n >= topk
        lo = jnp.where(ge, mid, lo)
        hi = jnp.where(n == topk, mid + 1, jnp.where(ge, hi, mid))
        return it + 1, unsettled(lo, hi), lo, hi

    lo0 = jnp.full((1, t), INT_MIN, i32)
    hi0 = jnp.where(keep_all, INT_MIN + 1, INT_MAX)
    _, _, lo, _ = lax.while_loop(bis_cond, bis_body, (jnp.int32(0), unsettled(lo0, hi0), lo0, hi0))
    tau = jnp.where(keep_all, INT_MIN + 1, lo)
    tau8 = rows8(tau)
    n_ge = count(lambda k, idx: k >= tau8)
    tie = jnp.logical_and(n_ge > topk, jnp.logical_not(keep_all))

    @pl.when(jnp.max(jnp.where(tie, 1, 0)) > 0)
    def _():
        n_eq_keep = topk - count(lambda k, idx: k > tau8)

        def bis_pos(_, carry):
            plo, phi = carry
            pmid = (plo + phi) >> 1
            pmid8 = rows8(pmid)
            ok = count(lambda k, idx: jnp.logical_and(k == tau8, idx < pmid8)) <= n_eq_keep
            return jnp.where(ok, pmid, plo), jnp.where(ok, phi, pmid)

        smax = pl.num_programs(1) * t
        nbits = int(math.ceil(math.log2(keys_ref.shape[0]))) + 1
        plo, _ = lax.fori_loop(0, nbits, bis_pos,
                               (jnp.zeros((1, t), i32), jnp.zeros((1, t), i32) + smax))
        cut = jnp.where(tie, plo, smax)

        def drop(j, c):
            off = tile_off(j)
            k = keys_ref[pl.ds(off, t), :]
            dead = jnp.logical_and(k == tau, (krow + j * t) >= cut)
            keys_ref[pl.ds(off, t), :] = jnp.where(dead, INT_MIN, k)
            return c

        lax.fori_loop(0, ntile, drop, 0)

    m_ref[...] = jnp.full(m_ref.shape, F32_MIN, jnp.float32)
    l_ref[...] = jnp.zeros(l_ref.shape, jnp.float32)
    acc_ref[...] = jnp.zeros(acc_ref.shape, jnp.float32)
    srow = lax.broadcasted_iota(i32, (LANES, t), 0)
    qmt = []
    for h in range(A_HEADS):
        pair = qat_ref[0, (h // 2) * LANES:(h // 2 + 1) * LANES, :]
        qmt.append(jnp.where((srow // A_HEAD_DIM) == (h % 2), pair, jnp.zeros_like(pair)))

    def attn_tile(j, bias_ref):
        off = tile_off(j)
        sel = keys_ref[pl.ds(off, t), :] >= tau
        for h in range(A_HEADS):
            g = h // 2
            s = _dot(k_ref[0, pl.ds(off, t), g * LANES:(g + 1) * LANES], qmt[h])
            if bias_ref is not None:
                s = s + bias_ref[h]
            s = jnp.where(sel, s, -jnp.inf)
            m_old = m_ref[h]
            m_new = jnp.maximum(m_old, jnp.max(s, axis=0, keepdims=True))
            alpha = jnp.exp(m_old - m_new)
            p = jnp.exp(s - m_new)
            l_ref[h] = alpha * l_ref[h] + jnp.sum(p, axis=0, keepdims=True)
            pv = _dot(vt_ref[0, h * A_HEAD_DIM:(h + 1) * A_HEAD_DIM, pl.ds(off, t)], p.astype(MXU_DTYPE))
            acc_ref[h] = alpha * acc_ref[h] + pv
            m_ref[h] = m_new

    def far_attn(j, c):
        attn_tile(j, None)
        return c

    lax.fori_loop(0, qblk - 1, far_attn, 0)

    @pl.when(qblk >= 1)
    def _():
        attn_tile(qblk - 1, bp_ref)

    attn_tile(qblk, bd_ref)

    out_t = jnp.concatenate([acc_ref[h] / l_ref[h] for h in range(A_HEADS)], axis=0)
    o_ref[0] = out_t.T.astype(o_ref.dtype)


def _dsa(qat, qit, wit, k, vt, ki, bd, bp, topk):
    b, _, s = qat.shape
    t = DSA_T
    qcols = lambda bi, qi_: (bi, 0, qi_)
    per_b = lambda bi, qi_: (bi, 0, 0)
    fix3 = lambda bi, qi_: (0, 0, 0)
    once = pl.Buffered(1)
    return pl.pallas_call(
        functools.partial(_dsa_kernel, topk=topk),
        grid=(b, s // t),
        in_specs=[
            pl.BlockSpec((1, A_WIDTH, t), qcols),
            pl.BlockSpec((1, IDX_HEADS * IDX_DIM, t), qcols),
            pl.BlockSpec((1, IDX_HEADS, t), qcols),
            pl.BlockSpec((1, s, A_WIDTH), per_b, pipeline_mode=once),
            pl.BlockSpec((1, A_WIDTH, s), per_b, pipeline_mode=once),
            pl.BlockSpec((1, s, IDX_DIM), per_b, pipeline_mode=once),
            pl.BlockSpec((A_HEADS, t, t), fix3, pipeline_mode=once),
            pl.BlockSpec((A_HEADS, t, t), fix3, pipeline_mode=once),
        ],
        out_specs=pl.BlockSpec((1, t, A_WIDTH), lambda bi, qi_: (bi, qi_, 0)),
        out_shape=jax.ShapeDtypeStruct((b, s, A_WIDTH), MXU_DTYPE),
        scratch_shapes=[
            pltpu.VMEM((s, t), jnp.int32),
            pltpu.VMEM((A_HEADS, 1, t), jnp.float32),
            pltpu.VMEM((A_HEADS, 1, t), jnp.float32),
            pltpu.VMEM((A_HEADS, A_HEAD_DIM, t), jnp.float32),
        ],
        compiler_params=pltpu.CompilerParams(dimension_semantics=("parallel", "arbitrary"),
                                             vmem_limit_bytes=VMEM_LIMIT),
        name="dsa",
    )(qat, qit, wit, k, vt, ki, bd, bp)


def _outproj_kernel(x_ref, oa_ref, ob_ref, wa_ref, wb_ref, g2_ref, wr_ref, br_ref,
                    x1_ref, h2_ref, comb_ref):
    x1 = x_ref[...] + _dot(oa_ref[...], wa_ref[...]) + _dot(ob_ref[...], wb_ref[...])
    x1_ref[...] = x1
    h2 = _rms(x1, g2_ref[...])
    h2m = h2.astype(MXU_DTYPE)
    h2_ref[...] = h2m
    logits = _dot(h2m, wr_ref[...]) + br_ref[...]
    lane = lax.broadcasted_iota(jnp.int32, logits.shape, 1)
    work = logits
    vals, hots = [], []
    for _ in range(TOP_K):
        mx = jnp.max(work, axis=1, keepdims=True)
        idx = jnp.min(jnp.where(work == mx, lane, LANES), axis=1, keepdims=True)
        hot = lane == idx
        vals.append(mx)
        hots.append(hot)
        work = jnp.where(hot, -jnp.inf, work)
    es = [jnp.exp(v - vals[0]) for v in vals]
    denom = es[0]
    for e in es[1:]:
        denom = denom + e
    comb = jnp.zeros(logits.shape, jnp.float32)
    for e, hot in zip(es, hots):
        comb = comb + jnp.where(hot, e / denom, 0.0)
    comb_ref[...] = comb


def _out_proj(x2, oa, ob, wa, wb, g2, wr, br):
    t, d = x2.shape
    tm = min(PROJ_TM, t)
    row = lambda i: (i, 0)
    fixed = lambda i: (0, 0)
    return pl.pallas_call(
        _outproj_kernel,
        grid=(t // tm,),
        in_specs=[
            pl.BlockSpec((tm, d), row),
            pl.BlockSpec((tm, A_WIDTH), row),
            pl.BlockSpec((tm, B_WIDTH), row),
            pl.BlockSpec((A_WIDTH, d), fixed),
            pl.BlockSpec((B_WIDTH, d), fixed),
            pl.BlockSpec((1, d), fixed),
            pl.BlockSpec((d, LANES), fixed),
            pl.BlockSpec((1, LANES), fixed),
        ],
        out_specs=[pl.BlockSpec((tm, d), row), pl.BlockSpec((tm, d), row), pl.BlockSpec((tm, LANES), row)],
        out_shape=[jax.ShapeDtypeStruct((t, d), jnp.float32),
                   jax.ShapeDtypeStruct((t, d), MXU_DTYPE),
                   jax.ShapeDtypeStruct((t, LANES), jnp.float32)],
        compiler_params=pltpu.CompilerParams(dimension_semantics=("parallel",),
                                             vmem_limit_bytes=VMEM_LIMIT),
        name="out_proj",
    )(x2, oa, ob, wa, wb, g2, wr, br)


def _moe_kernel(x1_ref, h2_ref, comb_ref, w1_ref, b1_ref, w2_ref, b2_ref, gf_ref, o_ref, acc_ref):
    e = pl.program_id(1)

    @pl.when(e == 0)
    def _():
        acc_ref[...] = jnp.zeros_like(acc_ref)

    comb = comb_ref[...]
    lane = lax.broadcasted_iota(jnp.int32, comb.shape, 1)
    c = jnp.sum(jnp.where(lane == e, comb, 0.0), axis=1, keepdims=True)
    h2 = h2_ref[...]
    even = (lax.broadcasted_iota(jnp.int32, (h2.shape[0], LANES), 1) % 2) == 0
    ff = w2_ref.shape[1]
    acts = []
    for m in range(ff // LANES):
        za = _dot(h2, w1_ref[0, :, 2 * m * LANES:(2 * m + 1) * LANES]) + b1_ref[0, :, 2 * m * LANES:(2 * m + 1) * LANES]
        zb = _dot(h2, w1_ref[0, :, (2 * m + 1) * LANES:(2 * m + 2) * LANES]) + b1_ref[0, :, (2 * m + 1) * LANES:(2 * m + 2) * LANES]
        glu = jnp.where(even, za, pltpu.roll(zb, 1, 1))
        lin = jnp.where(even, pltpu.roll(za, LANES - 1, 1), zb)
        glu = jnp.minimum(glu, SWIGLU_LIMIT)
        lin = jnp.clip(lin, -SWIGLU_LIMIT, SWIGLU_LIMIT)
        act = glu * (1.0 / (1.0 + jnp.exp(-SWIGLU_ALPHA * glu))) * (lin + 1.0)
        acts.append(act.astype(MXU_DTYPE))
    y = _dot(jnp.concatenate(acts, axis=1), w2_ref[0]) + b2_ref[0]
    acc_ref[...] += c * y

    @pl.when(e == pl.num_programs(1) - 1)
    def _():
        o_ref[...] = _rms(x1_ref[...] + acc_ref[...], gf_ref[...])


def _pair_rows(w2):
    ne, ff, d = w2.shape
    half = LANES // 2
    return w2.reshape(ne, ff // LANES, 2, half, d).swapaxes(2, 3).reshape(ne, ff, d)


def _moe(x1, h2, comb, w1, b1, w2, b2, gf):
    t, d = x1.shape
    ne, _, f2 = w1.shape
    ff = f2 // 2
    tm = min(MOE_TM, t)
    row = lambda i, e: (i, 0)
    return pl.pallas_call(
        _moe_kernel,
        grid=(t // tm, ne),
        in_specs=[
            pl.BlockSpec((tm, d), row),
            pl.BlockSpec((tm, d), row),
            pl.BlockSpec((tm, LANES), row),
            pl.BlockSpec((1, d, f2), lambda i, e: (e, 0, 0)),
            pl.BlockSpec((1, 1, f2), lambda i, e: (e, 0, 0)),
            pl.BlockSpec((1, ff, d), lambda i, e: (e, 0, 0)),
            pl.BlockSpec((1, 1, d), lambda i, e: (e, 0, 0)),
            pl.BlockSpec((1, d), lambda i, e: (0, 0)),
        ],
        out_specs=pl.BlockSpec((tm, d), row),
        out_shape=jax.ShapeDtypeStruct((t, d), jnp.float32),
        scratch_shapes=[pltpu.VMEM((tm, d), jnp.float32)],
        compiler_params=pltpu.CompilerParams(dimension_semantics=("parallel", "arbitrary"),
                                             vmem_limit_bytes=VMEM_LIMIT),
        name="moe",
    )(x1, h2, comb, w1, b1, w2, b2, gf)


def kernel(x, norm1_gain, w_in, rel_bias, idx_k_ln_gain, idx_k_ln_bias, ret_gn_gain,
           w_out, norm2_gain, w_router, b_router, w_mlp1, b_mlp1, w_mlp2, b_mlp2, final_gain):
    bsz, s, d = x.shape
    depth = w_in.shape[0]
    topk = min(IDX_TOPK_MAX, s // 4)
    assert depth == 1, "the final RMSNorm is fused into the expert kernel of the single layer"
    assert s % DSA_T == 0 and s % RET_C == 0 and topk <= DSA_T and DSA_T % CHUNK == 0
    ne = w_router.shape[-1]
    assert ne <= LANES
    f32 = jnp.float32
    x2 = x.reshape(bsz * s, d).astype(f32)
    cos, sin = _rotary_tables(s)
    bd, bp = _bias_tiles(rel_bias, DSA_T)

    for layer in range(depth):
        wp = _pack_proj_weight(w_in[layer])
        qa, ka, va, qi, ki, wi, qb, kb, vb, sg = _in_proj(
            x2, norm1_gain[layer][None, :].astype(f32), wp, cos, sin,
            idx_k_ln_gain[layer][None, :].astype(f32), idx_k_ln_bias[layer][None, :].astype(f32), s)

        r3 = lambda a: a.reshape(bsz, s, a.shape[-1])
        out_b = _retention(r3(qb), r3(kb), r3(vb), r3(sg),
                           ret_gn_gain[layer][None, :].astype(f32), min(RET_C, s))
        tr = lambda a: jnp.swapaxes(r3(a), 1, 2)
        out_a = _dsa(tr(qa), tr(qi), tr(wi), r3(ka), tr(va), r3(ki), bd, bp, topk)

        wo = w_out[layer].astype(MXU_DTYPE)
        wr = jnp.pad(w_router[layer], ((0, 0), (0, LANES - ne))).astype(MXU_DTYPE)
        br = jnp.pad(b_router[layer].astype(f32), (0, LANES - ne), constant_values=-jnp.inf)[None, :]
        x1, h2, comb = _out_proj(x2, out_a.reshape(bsz * s, A_WIDTH), out_b.reshape(bsz * s, B_WIDTH),
                                 wo[:A_WIDTH], wo[A_WIDTH:], norm2_gain[layer][None, :].astype(f32), wr, br)

        x2 = _moe(x1, h2, comb, w_mlp1[layer].astype(MXU_DTYPE), b_mlp1[layer].astype(f32)[:, None, :],
                  _pair_rows(w_mlp2[layer]).astype(MXU_DTYPE),
                  b_mlp2[layer].astype(f32)[:, None, :], final_gain[None, :].astype(f32))

    return x2.reshape(bsz, s, d).astype(x.dtype)
```

```python
import functools
import math

import numpy as np
import jax
import jax.numpy as jnp
from jax import lax
from jax.experimental import pallas as pl
from jax.experimental.pallas import tpu as pltpu

CHUNK = 64
A_HEADS = 8
A_HEAD_DIM = 64
A_WIDTH = A_HEADS * A_HEAD_DIM
IDX_HEADS = 8
IDX_DIM = 64
IDX_TOPK_MAX = 256
B_HEADS = 4
B_QK_DIM = 64
B_V_DIM = 128
B_QK_WIDTH = B_HEADS * B_QK_DIM
B_WIDTH = B_HEADS * B_V_DIM
REL_BUCKETS = 32
REL_MAX_DIST = 128
TOP_K = 4
SWIGLU_LIMIT = 7.0
SWIGLU_ALPHA = 1.702
ROPE_BASE = 10000.0
RMS_EPS = 1e-5
LN_EPS = 1e-6

LANES = 128
SUBLANES = 8
VMEM_LIMIT = 56 * 1024 * 1024

MXU_DTYPE = jnp.bfloat16

INT_MIN = -(2 ** 31)
INT_MAX = 2 ** 31 - 1
F32_MIN = float(np.finfo(np.float32).min)
LOG2E = math.log2(math.e)

PROJ_TM = 512
RET_C = 256
DSA_T = 256
MOE_TM = 512


def _dot(a, b):
    return jnp.dot(a, b, preferred_element_type=jnp.float32)


def _dot_nt(a, b):
    return lax.dot_general(a, b, (((1,), (1,)), ((), ())), preferred_element_type=jnp.float32)


def _rms(x, gain):
    return x * lax.rsqrt(jnp.mean(x * x, axis=-1, keepdims=True) + RMS_EPS) * gain


_SEG = {}
_off = 0
for _name, _w in (("qa", A_WIDTH), ("ka", A_WIDTH), ("va", A_WIDTH), ("qi", IDX_HEADS * IDX_DIM),
                  ("ki", LANES), ("wi", LANES),
                  ("qb", B_HEADS * LANES), ("qbr", B_HEADS * LANES),
                  ("kb", B_HEADS * LANES), ("kbr", B_HEADS * LANES),
                  ("vb", B_WIDTH), ("gb", B_WIDTH)):
    _SEG[_name] = (_off, _off + _w)
    _off += _w
PROJ_N = _off


def _proj_kernel(x_ref, g_ref, w_ref, cos_ref, sin_ref, lng_ref, lnb_ref,
                 qa_ref, ka_ref, va_ref, qi_ref, ki_ref, wi_ref, qb_ref, kb_ref, vb_ref, sg_ref):
    h = _rms(x_ref[...], g_ref[...]).astype(MXU_DTYPE)

    def seg(name):
        a, b = _SEG[name]
        return _dot(h, w_ref[:, a:b])

    qa_ref[...] = (seg("qa") * (A_HEAD_DIM ** -0.5 * LOG2E)).astype(qa_ref.dtype)
    ka_ref[...] = seg("ka").astype(ka_ref.dtype)
    va_ref[...] = seg("va").astype(va_ref.dtype)
    qi_ref[...] = seg("qi").astype(qi_ref.dtype)
    ki = seg("ki")[:, :IDX_DIM]
    mu = jnp.mean(ki, axis=-1, keepdims=True)
    var = jnp.mean(jnp.square(ki - mu), axis=-1, keepdims=True)
    ki = (ki - mu) * lax.rsqrt(var + LN_EPS) * lng_ref[...] + lnb_ref[...]
    ki_ref[...] = ki.astype(ki_ref.dtype)
    wi_ref[...] = seg("wi")[:, :IDX_HEADS] * ((IDX_HEADS ** -0.5) * (IDX_DIM ** -0.5))
    cos = cos_ref[...]
    sin = sin_ref[...]
    qb_ref[...] = (seg("qb") * cos + seg("qbr") * sin).astype(qb_ref.dtype)
    kb_ref[...] = ((seg("kb") * cos + seg("kbr") * sin) * (B_QK_DIM ** -0.5)).astype(kb_ref.dtype)
    vb_ref[...] = seg("vb").astype(vb_ref.dtype)
    g = seg("gb")
    sg_ref[...] = g / (1.0 + jnp.exp(-g))


def _pack_proj_weight(w_in):
    d = w_in.shape[0]
    sizes = (A_WIDTH, A_WIDTH, A_WIDTH, IDX_HEADS * IDX_DIM, IDX_DIM, IDX_HEADS,
             B_QK_WIDTH, B_QK_WIDTH, B_WIDTH, B_WIDTH)
    cuts = np.cumsum(sizes)[:-1].tolist()
    wqa, wka, wva, wqi, wki, wwi, wqb, wkb, wvb, wgb = jnp.split(w_in, cuts, axis=-1)

    def pad_cols(w, n):
        return jnp.pad(w, ((0, 0), (0, n - w.shape[1])))

    def head_slots(w):
        w = w.reshape(d, B_HEADS, B_QK_DIM)
        return jnp.pad(w, ((0, 0), (0, 0), (0, LANES - B_QK_DIM))).reshape(d, B_HEADS * LANES)

    def rot_half(w):
        w = w.reshape(d, B_HEADS, 2, B_QK_DIM // 2)
        return jnp.stack([-w[:, :, 1], w[:, :, 0]], axis=2).reshape(d, B_QK_WIDTH)

    parts = [wqa, wka, wva, wqi, pad_cols(wki, LANES), pad_cols(wwi, LANES),
             head_slots(wqb), head_slots(rot_half(wqb)), head_slots(wkb), head_slots(rot_half(wkb)),
             wvb, wgb]
    return jnp.concatenate(parts, axis=-1).astype(MXU_DTYPE)


def _rotary_tables(s):
    half = B_QK_DIM // 2
    inv = 1.0 / (ROPE_BASE ** jnp.linspace(0.0, 1.0, half, dtype=jnp.float32))
    ang = jnp.arange(s, dtype=jnp.int32).astype(jnp.float32)[:, None] * inv[None, :]

    def slots(t):
        t = jnp.concatenate([t, t, jnp.zeros((s, LANES - B_QK_DIM), jnp.float32)], axis=-1)
        return jnp.tile(t, (1, B_HEADS))

    return slots(jnp.cos(ang)), slots(jnp.sin(ang))


def _in_proj(x2, g1, wp, cos, sin, lng, lnb, s):
    t, d = x2.shape
    tm = min(PROJ_TM, s)
    nt = t // tm
    ns = s // tm
    row = lambda i: (i, 0)
    fixed = lambda i: (0, 0)
    pos = lambda i: (i % ns, 0)
    bw = B_HEADS * LANES
    out_shapes = [
        jax.ShapeDtypeStruct((t, A_WIDTH), MXU_DTYPE),
        jax.ShapeDtypeStruct((t, A_WIDTH), MXU_DTYPE),
        jax.ShapeDtypeStruct((t, A_WIDTH), MXU_DTYPE),
        jax.ShapeDtypeStruct((t, IDX_HEADS * IDX_DIM), MXU_DTYPE),
        jax.ShapeDtypeStruct((t, IDX_DIM), MXU_DTYPE),
        jax.ShapeDtypeStruct((t, IDX_HEADS), jnp.float32),
        jax.ShapeDtypeStruct((t, bw), MXU_DTYPE),
        jax.ShapeDtypeStruct((t, bw), MXU_DTYPE),
        jax.ShapeDtypeStruct((t, B_WIDTH), MXU_DTYPE),
        jax.ShapeDtypeStruct((t, B_WIDTH), jnp.float32),
    ]
    out_specs = [pl.BlockSpec((tm, o.shape[1]), row) for o in out_shapes]
    return pl.pallas_call(
        _proj_kernel,
        grid=(nt,),
        in_specs=[
            pl.BlockSpec((tm, d), row),
            pl.BlockSpec((1, d), fixed),
            pl.BlockSpec((d, PROJ_N), fixed),
            pl.BlockSpec((tm, bw), pos),
            pl.BlockSpec((tm, bw), pos),
            pl.BlockSpec((1, IDX_DIM), fixed),
            pl.BlockSpec((1, IDX_DIM), fixed),
        ],
        out_specs=out_specs,
        out_shape=out_shapes,
        compiler_params=pltpu.CompilerParams(dimension_semantics=("parallel",),
                                             vmem_limit_bytes=VMEM_LIMIT),
        name="in_proj",
    )(x2, g1, wp, cos, sin, lng, lnb)


def _ret_kernel(q_ref, k_ref, v_ref, sg_ref, gain_ref, decay_ref, zeta_ref, xi_ref, cd_ref,
                o_ref, state_ref):
    @pl.when(pl.program_id(1) == 0)
    def _():
        state_ref[...] = jnp.zeros_like(state_ref)

    for h in range(B_HEADS):
        sl = slice(h * LANES, (h + 1) * LANES)
        q = q_ref[0, :, sl]
        k = k_ref[0, :, sl]
        v = v_ref[0, :, sl]
        state = state_ref[h]
        scores = _dot_nt(q, k) * decay_ref[h]
        intra = _dot(scores.astype(MXU_DTYPE), v)
        cross = _dot(q, state.astype(MXU_DTYPE)) * xi_ref[h]
        y = intra + cross
        kz = (k.astype(jnp.float32) * zeta_ref[h]).T.astype(MXU_DTYPE)
        state_ref[h] = cd_ref[h] * state + _dot(kz, v)
        mu = jnp.mean(y, axis=-1, keepdims=True)
        var = jnp.mean(jnp.square(y - mu), axis=-1, keepdims=True)
        yn = (y - mu) * lax.rsqrt(var + LN_EPS) * gain_ref[:, sl]
        o_ref[0, :, sl] = (sg_ref[0, :, sl] * yn).astype(o_ref.dtype)


def _retention(qb, kb, vb, sg, gain, c):
    b, s, bw = qb.shape
    f32 = jnp.float32
    log_g = jnp.log(1.0 - 2.0 ** (-5.0 - jnp.arange(B_HEADS, dtype=f32)))
    n = jnp.arange(c, dtype=f32)
    diff = n[:, None] - n[None, :]
    decay = jnp.where(diff >= 0, jnp.exp(log_g[:, None, None] * jnp.maximum(diff, 0.0)), 0.0)
    zeta = jnp.exp(log_g[:, None] * (c - 1.0 - n)[None, :])[:, :, None]
    xi = jnp.exp(log_g[:, None] * (n + 1.0)[None, :])[:, :, None]
    cd = jnp.broadcast_to(jnp.exp(log_g * c)[:, None, None], (B_HEADS, 1, LANES))
    blk = lambda bi, ci: (bi, ci, 0)
    fix3 = lambda bi, ci: (0, 0, 0)
    return pl.pallas_call(
        _ret_kernel,
        grid=(b, s // c),
        in_specs=[
            pl.BlockSpec((1, c, bw), blk),
            pl.BlockSpec((1, c, bw), blk),
            pl.BlockSpec((1, c, B_WIDTH), blk),
            pl.BlockSpec((1, c, B_WIDTH), blk),
            pl.BlockSpec((1, B_WIDTH), lambda bi, ci: (0, 0)),
            pl.BlockSpec((B_HEADS, c, c), fix3),
            pl.BlockSpec((B_HEADS, c, 1), fix3),
            pl.BlockSpec((B_HEADS, c, 1), fix3),
            pl.BlockSpec((B_HEADS, 1, LANES), fix3),
        ],
        out_specs=pl.BlockSpec((1, c, B_WIDTH), blk),
        out_shape=jax.ShapeDtypeStruct((b, s, B_WIDTH), MXU_DTYPE),
        scratch_shapes=[pltpu.VMEM((B_HEADS, LANES, LANES), jnp.float32)],
        compiler_params=pltpu.CompilerParams(dimension_semantics=("parallel", "arbitrary"),
                                             vmem_limit_bytes=VMEM_LIMIT),
        name="retention",
    )(qb, kb, vb, sg, gain, decay, zeta, xi, cd)


def _t5_bucket(rel):
    nb = REL_BUCKETS // 2
    max_exact = nb // 2
    ret = jnp.where(rel > 0, nb, 0)
    n = jnp.abs(rel)
    nf = jnp.maximum(n, 1).astype(jnp.float32)
    large = max_exact + (jnp.log(nf / max_exact) / math.log(REL_MAX_DIST / max_exact)
                         * (nb - max_exact)).astype(jnp.int32)
    large = jnp.minimum(large, nb - 1)
    return ret + jnp.where(n < max_exact, n, large)


def _bias_tiles(rel_bias, t):
    i = jnp.arange(t, dtype=jnp.int32)
    rel_d = i[:, None] - i[None, :]
    rb = rel_bias.astype(jnp.float32)
    far = rb[_t5_bucket(jnp.int32(-(t + 1)))]

    def tile(rel):
        hot = jax.nn.one_hot(_t5_bucket(rel), REL_BUCKETS, dtype=jnp.float32)
        return (jnp.einsum("kqb,bh->hkq", hot, rb, precision=lax.Precision.HIGHEST) - far[:, None, None]) * LOG2E

    near = tile(rel_d - t)
    return jnp.stack([jnp.zeros_like(near), near, tile(rel_d)])


def _dsa_kernel(qat_ref, qit_ref, wit_ref, k_ref, vt_ref, ki_ref, bias_ref, o_ref,
                keys_ref, mb_ref, sbuf_ref, m_ref, al_ref, l_ref, acc_ref, *, topk):
    t = DSA_T
    i32 = jnp.int32
    qblk = pl.program_id(1)
    ntile = qblk + 1
    qcol = lax.broadcasted_iota(i32, (1, t), 1)
    lim_local = (qcol // CHUNK + 1) * CHUNK
    keep_all = (qblk * t + lim_local) <= topk
    krow = lax.broadcasted_iota(i32, (t, t), 0)

    def tile_off(j):
        return pl.multiple_of(j * t, t)

    qit = [qit_ref[0, h * IDX_DIM:(h + 1) * IDX_DIM, :] for h in range(IDX_HEADS)]
    wit = [wit_ref[0, h:h + 1, :] for h in range(IDX_HEADS)]

    def score_tile(j, diag):
        off = tile_off(j)
        kt = ki_ref[0, pl.ds(off, t), :]
        sc = jnp.zeros((t, t), jnp.float32)
        for h in range(IDX_HEADS):
            sc = sc + wit[h] * jnp.maximum(_dot(kt, qit[h]), 0.0)
        sc = jnp.where(sc == 0.0, 0.0, sc)
        bits = pltpu.bitcast(sc, i32)
        key = bits ^ ((bits >> 31) & INT_MAX)
        if diag:
            key = jnp.where(krow < lim_local, key, INT_MIN)
        keys_ref[pl.ds(off, t), :] = key

    def far_scores(j, c):
        score_tile(j, False)
        return c

    lax.fori_loop(0, qblk, far_scores, 0)
    score_tile(qblk, True)

    sub = lax.broadcasted_iota(i32, (SUBLANES, t), 0)

    def rows8(v):
        return jnp.broadcast_to(v, (SUBLANES, t))

    def count(pred):
        def body(j, cnt):
            off = tile_off(j)
            tile = keys_ref[pl.ds(off, t), :]
            for r in range(0, t, SUBLANES):
                cnt = cnt + jnp.where(pred(tile[r:r + SUBLANES], sub + (off + r)), 1, 0)
            return cnt
        cnt = lax.fori_loop(0, ntile, body, jnp.zeros((SUBLANES, t), i32))
        return jnp.sum(cnt, axis=0, keepdims=True)

    def unsettled(lo, hi):
        return jnp.max(jnp.where(hi != lo + 1, 1.0, 0.0))

    def bis_cond(c):
        it, active, _, _ = c
        return jnp.logical_and(it < 32, active > 0.0)

    def bis_pass(lo, hi):
        mid = (lo >> 1) + (hi >> 1) + (lo & hi & 1)
        mid8 = rows8(mid)
        n = count(lambda k, idx: k >= mid8)
        ge = n >= topk
        lo = jnp.where(ge, mid, lo)
        hi = jnp.where(n == topk, mid + 1, jnp.where(ge, hi, mid))
        return lo, hi

    def bis_body(c):
        it, _, lo, hi = c
        lo, hi = bis_pass(*bis_pass(lo, hi))
        return it + 2, unsettled(lo, hi), lo, hi

    lo0 = jnp.full((1, t), INT_MIN, i32)
    hi0 = jnp.where(keep_all, INT_MIN + 1, INT_MAX)
    _, _, lo, _ = lax.while_loop(bis_cond, bis_body, (jnp.int32(0), unsettled(lo0, hi0), lo0, hi0))
    tau = jnp.where(keep_all, INT_MIN + 1, lo)
    tau8 = rows8(tau)
    n_ge = count(lambda k, idx: k >= tau8)
    tie = jnp.logical_and(n_ge > topk, jnp.logical_not(keep_all))

    @pl.when(jnp.max(jnp.where(tie, 1, 0)) > 0)
    def _():
        n_eq_keep = topk - count(lambda k, idx: k > tau8)

        def bis_pos(_, carry):
            plo, phi = carry
            pmid = (plo + phi) >> 1
            pmid8 = rows8(pmid)
            ok = count(lambda k, idx: jnp.logical_and(k == tau8, idx < pmid8)) <= n_eq_keep
            return jnp.where(ok, pmid, plo), jnp.where(ok, phi, pmid)

        smax = pl.num_programs(1) * t
        nbits = int(math.ceil(math.log2(keys_ref.shape[0]))) + 1
        plo, _ = lax.fori_loop(0, nbits, bis_pos,
                               (jnp.zeros((1, t), i32), jnp.zeros((1, t), i32) + smax))
        cut = jnp.where(tie, plo, smax)

        def drop(j, c):
            off = tile_off(j)
            k = keys_ref[pl.ds(off, t), :]
            dead = jnp.logical_and(k == tau, (krow + j * t) >= cut)
            keys_ref[pl.ds(off, t), :] = jnp.where(dead, INT_MIN, k)
            return c

        lax.fori_loop(0, ntile, drop, 0)

    m_ref[...] = jnp.full(m_ref.shape, F32_MIN, jnp.float32)
    al_ref[...] = jnp.ones(al_ref.shape, jnp.float32)
    l_ref[...] = jnp.zeros(l_ref.shape, jnp.float32)
    acc_ref[...] = jnp.zeros(acc_ref.shape, jnp.float32)
    srow = lax.broadcasted_iota(i32, (LANES, t), 0)
    qmt = []
    for h in range(A_HEADS):
        pair = qat_ref[0, (h // 2) * LANES:(h // 2 + 1) * LANES, :]
        qmt.append(jnp.where((srow // A_HEAD_DIM) == (h % 2), pair, jnp.zeros_like(pair)))

    def bias_kind(j):
        return jnp.where(j >= qblk, 2, jnp.where(j >= qblk - 1, 1, 0))

    def set_mask(j):
        mb_ref[...] = jnp.where(keys_ref[pl.ds(tile_off(j), t), :] >= tau, 0.0, -jnp.inf)

    def consume(j, h):
        m_h = m_ref[h]
        a_h = al_ref[h]
        p = jnp.exp2(sbuf_ref[h] - m_h)
        l_ref[h] = a_h * l_ref[h] + jnp.sum(p, axis=0, keepdims=True)
        pv = _dot(vt_ref[0, h * A_HEAD_DIM:(h + 1) * A_HEAD_DIM, pl.ds(tile_off(j), t)], p.astype(MXU_DTYPE))
        acc_ref[h] = a_h * acc_ref[h] + pv

    def produce(j, h, kind):
        g = h // 2
        s = _dot(k_ref[0, pl.ds(tile_off(j), t), g * LANES:(g + 1) * LANES], qmt[h]) + mb_ref[...]
        if kind is not None:
            s = s + bias_ref[kind, h]
        m_h = m_ref[h]
        m_new = jnp.maximum(m_h, jnp.max(s, axis=0, keepdims=True))
        al_ref[h] = jnp.exp2(m_h - m_new)
        m_ref[h] = m_new
        sbuf_ref[h] = s

    set_mask(0)
    for h in range(A_HEADS):
        produce(0, h, bias_kind(0))

    def step(j, kind):
        set_mask(j + 1)
        for h in range(A_HEADS):
            consume(j, h)
            produce(j + 1, h, kind)

    def far_step(j, c):
        step(j, None)
        return c

    def near_step(j, c):
        step(j, bias_kind(j + 1))
        return c

    n_far = jnp.maximum(qblk - 2, 0)
    lax.fori_loop(0, n_far, far_step, 0)
    lax.fori_loop(n_far, qblk, near_step, 0)
    for h in range(A_HEADS):
        consume(qblk, h)

    out_t = jnp.concatenate([acc_ref[h] / l_ref[h] for h in range(A_HEADS)], axis=0)
    o_ref[0] = out_t.T.astype(o_ref.dtype)


def _dsa(qat, qit, wit, k, vt, ki, bias, topk):
    b, _, s = qat.shape
    t = DSA_T
    qcols = lambda bi, qi_: (bi, 0, qi_)
    per_b = lambda bi, qi_: (bi, 0, 0)
    once = pl.Buffered(1)
    return pl.pallas_call(
        functools.partial(_dsa_kernel, topk=topk),
        grid=(b, s // t),
        in_specs=[
            pl.BlockSpec((1, A_WIDTH, t), qcols),
            pl.BlockSpec((1, IDX_HEADS * IDX_DIM, t), qcols),
            pl.BlockSpec((1, IDX_HEADS, t), qcols),
            pl.BlockSpec((1, s, A_WIDTH), per_b, pipeline_mode=once),
            pl.BlockSpec((1, A_WIDTH, s), per_b, pipeline_mode=once),
            pl.BlockSpec((1, s, IDX_DIM), per_b, pipeline_mode=once),
            pl.BlockSpec((3, A_HEADS, t, t), lambda bi, qi_: (0, 0, 0, 0), pipeline_mode=once),
        ],
        out_specs=pl.BlockSpec((1, t, A_WIDTH), lambda bi, qi_: (bi, qi_, 0)),
        out_shape=jax.ShapeDtypeStruct((b, s, A_WIDTH), MXU_DTYPE),
        scratch_shapes=[
            pltpu.VMEM((s, t), jnp.int32),
            pltpu.VMEM((t, t), jnp.float32),
            pltpu.VMEM((A_HEADS, t, t), jnp.float32),
            pltpu.VMEM((A_HEADS, 1, t), jnp.float32),
            pltpu.VMEM((A_HEADS, 1, t), jnp.float32),
            pltpu.VMEM((A_HEADS, 1, t), jnp.float32),
            pltpu.VMEM((A_HEADS, A_HEAD_DIM, t), jnp.float32),
        ],
        compiler_params=pltpu.CompilerParams(dimension_semantics=("parallel", "arbitrary"),
                                             vmem_limit_bytes=VMEM_LIMIT),
        name="dsa",
    )(qat, qit, wit, k, vt, ki, bias)


def _outproj_kernel(x_ref, oa_ref, ob_ref, wa_ref, wb_ref, g2_ref, wr_ref, br_ref,
                    x1_ref, h2_ref, comb_ref):
    x1 = x_ref[...] + _dot(oa_ref[...], wa_ref[...]) + _dot(ob_ref[...], wb_ref[...])
    x1_ref[...] = x1
    h2 = _rms(x1, g2_ref[...])
    h2m = h2.astype(MXU_DTYPE)
    h2_ref[...] = h2m
    logits = _dot(h2m, wr_ref[...]) + br_ref[...]
    lane = lax.broadcasted_iota(jnp.int32, logits.shape, 1)
    work = logits
    vals, hots = [], []
    for _ in range(TOP_K):
        mx = jnp.max(work, axis=1, keepdims=True)
        idx = jnp.min(jnp.where(work == mx, lane, LANES), axis=1, keepdims=True)
        hot = lane == idx
        vals.append(mx)
        hots.append(hot)
        work = jnp.where(hot, -jnp.inf, work)
    es = [jnp.exp(v - vals[0]) for v in vals]
    denom = es[0]
    for e in es[1:]:
        denom = denom + e
    comb = jnp.zeros(logits.shape, jnp.float32)
    for e, hot in zip(es, hots):
        comb = comb + jnp.where(hot, e / denom, 0.0)
    comb_ref[...] = comb


def _out_proj(x2, oa, ob, wa, wb, g2, wr, br):
    t, d = x2.shape
    tm = min(PROJ_TM, t)
    row = lambda i: (i, 0)
    fixed = lambda i: (0, 0)
    return pl.pallas_call(
        _outproj_kernel,
        grid=(t // tm,),
        in_specs=[
            pl.BlockSpec((tm, d), row),
            pl.BlockSpec((tm, A_WIDTH), row),
            pl.BlockSpec((tm, B_WIDTH), row),
            pl.BlockSpec((A_WIDTH, d), fixed),
            pl.BlockSpec((B_WIDTH, d), fixed),
            pl.BlockSpec((1, d), fixed),
            pl.BlockSpec((d, LANES), fixed),
            pl.BlockSpec((1, LANES), fixed),
        ],
        out_specs=[pl.BlockSpec((tm, d), row), pl.BlockSpec((tm, d), row), pl.BlockSpec((tm, LANES), row)],
        out_shape=[jax.ShapeDtypeStruct((t, d), jnp.float32),
                   jax.ShapeDtypeStruct((t, d), MXU_DTYPE),
                   jax.ShapeDtypeStruct((t, LANES), jnp.float32)],
        compiler_params=pltpu.CompilerParams(dimension_semantics=("parallel",),
                                             vmem_limit_bytes=VMEM_LIMIT),
        name="out_proj",
    )(x2, oa, ob, wa, wb, g2, wr, br)


def _moe_kernel(x1_ref, h2_ref, comb_ref, w1_ref, b1_ref, w2_ref, b2_ref, gf_ref, o_ref, acc_ref):
    e = pl.program_id(1)

    @pl.when(e == 0)
    def _():
        acc_ref[...] = jnp.zeros_like(acc_ref)

    comb = comb_ref[...]
    lane = lax.broadcasted_iota(jnp.int32, comb.shape, 1)
    c = jnp.sum(jnp.where(lane == e, comb, 0.0), axis=1, keepdims=True)
    h2 = h2_ref[...]
    even = (lax.broadcasted_iota(jnp.int32, (h2.shape[0], LANES), 1) % 2) == 0
    ff = w2_ref.shape[1]
    z = _dot(h2, w1_ref[0]) + b1_ref[0]
    acts = []
    for m in range(ff // LANES):
        za = z[:, 2 * m * LANES:(2 * m + 1) * LANES]
        zb = z[:, (2 * m + 1) * LANES:(2 * m + 2) * LANES]
        glu = jnp.where(even, za, pltpu.roll(zb, 1, 1))
        lin = jnp.where(even, pltpu.roll(za, LANES - 1, 1), zb)
        glu = jnp.minimum(glu, SWIGLU_LIMIT)
        lin = jnp.clip(lin, -SWIGLU_LIMIT, SWIGLU_LIMIT)
        act = glu * (1.0 / (1.0 + jnp.exp(-SWIGLU_ALPHA * glu))) * (lin + 1.0)
        acts.append(act.astype(MXU_DTYPE))
    y = _dot(jnp.concatenate(acts, axis=1), w2_ref[0]) + b2_ref[0]
    acc_ref[...] += c * y

    @pl.when(e == pl.num_programs(1) - 1)
    def _():
        o_ref[...] = _rms(x1_ref[...] + acc_ref[...], gf_ref[...])


def _pair_rows(w2):
    ne, ff, d = w2.shape
    half = LANES // 2
    return w2.reshape(ne, ff // LANES, 2, half, d).swapaxes(2, 3).reshape(ne, ff, d)


def _moe(x1, h2, comb, w1, b1, w2, b2, gf):
    t, d = x1.shape
    ne, _, f2 = w1.shape
    ff = f2 // 2
    tm = min(MOE_TM, t)
    row = lambda i, e: (i, 0)
    return pl.pallas_call(
        _moe_kernel,
        grid=(t // tm, ne),
        in_specs=[
            pl.BlockSpec((tm, d), row),
            pl.BlockSpec((tm, d), row),
            pl.BlockSpec((tm, LANES), row),
            pl.BlockSpec((1, d, f2), lambda i, e: (e, 0, 0)),
            pl.BlockSpec((1, 1, f2), lambda i, e: (e, 0, 0)),
            pl.BlockSpec((1, ff, d), lambda i, e: (e, 0, 0)),
            pl.BlockSpec((1, 1, d), lambda i, e: (e, 0, 0)),
            pl.BlockSpec((1, d), lambda i, e: (0, 0)),
        ],
        out_specs=pl.BlockSpec((tm, d), row),
        out_shape=jax.ShapeDtypeStruct((t, d), jnp.float32),
        scratch_shapes=[pltpu.VMEM((tm, d), jnp.float32)],
        compiler_params=pltpu.CompilerParams(dimension_semantics=("parallel", "arbitrary"),
                                             vmem_limit_bytes=VMEM_LIMIT),
        name="moe",
    )(x1, h2, comb, w1, b1, w2, b2, gf)


def kernel(x, norm1_gain, w_in, rel_bias, idx_k_ln_gain, idx_k_ln_bias, ret_gn_gain,
           w_out, norm2_gain, w_router, b_router, w_mlp1, b_mlp1, w_mlp2, b_mlp2, final_gain):
    bsz, s, d = x.shape
    depth = w_in.shape[0]
    topk = min(IDX_TOPK_MAX, s // 4)
    assert depth == 1, "the final RMSNorm is fused into the expert kernel of the single layer"
    assert s % DSA_T == 0 and s % RET_C == 0 and topk <= DSA_T and DSA_T % CHUNK == 0
    assert s % min(PROJ_TM, s) == 0 and (bsz * s) % MOE_TM == 0
    ne = w_router.shape[-1]
    assert ne <= LANES
    f32 = jnp.float32
    x2 = x.reshape(bsz * s, d).astype(f32)
    cos, sin = _rotary_tables(s)
    bias = _bias_tiles(rel_bias, DSA_T)

    for layer in range(depth):
        wp = _pack_proj_weight(w_in[layer])
        qa, ka, va, qi, ki, wi, qb, kb, vb, sg = _in_proj(
            x2, norm1_gain[layer][None, :].astype(f32), wp, cos, sin,
            idx_k_ln_gain[layer][None, :].astype(f32), idx_k_ln_bias[layer][None, :].astype(f32), s)

        r3 = lambda a: a.reshape(bsz, s, a.shape[-1])
        out_b = _retention(r3(qb), r3(kb), r3(vb), r3(sg),
                           ret_gn_gain[layer][None, :].astype(f32), min(RET_C, s))
        tr = lambda a: jnp.swapaxes(r3(a), 1, 2)
        out_a = _dsa(tr(qa), tr(qi), tr(wi), r3(ka), tr(va), r3(ki), bias, topk)

        wo = w_out[layer].astype(MXU_DTYPE)
        wr = jnp.pad(w_router[layer], ((0, 0), (0, LANES - ne))).astype(MXU_DTYPE)
        br = jnp.pad(b_router[layer].astype(f32), (0, LANES - ne), constant_values=-jnp.inf)[None, :]
        x1, h2, comb = _out_proj(x2, out_a.reshape(bsz * s, A_WIDTH), out_b.reshape(bsz * s, B_WIDTH),
                                 wo[:A_WIDTH], wo[A_WIDTH:], norm2_gain[layer][None, :].astype(f32), wr, br)

        x2 = _moe(x1, h2, comb, w_mlp1[layer].astype(MXU_DTYPE), b_mlp1[layer].astype(f32)[:, None, :],
                  _pair_rows(w_mlp2[layer]).astype(MXU_DTYPE),
                  b_mlp2[layer].astype(f32)[:, None, :], final_gain[None, :].astype(f32))

    return x2.reshape(bsz, s, d).astype(x.dtype)
```

```python
import functools
import math

import numpy as np
import jax
import jax.numpy as jnp
from jax import lax
from jax.experimental import pallas as pl
from jax.experimental.pallas import tpu as pltpu
from jax.experimental.pallas import tpu_sc as plsc

CHUNK = 64
A_HEADS = 8
A_HEAD_DIM = 64
A_WIDTH = A_HEADS * A_HEAD_DIM
IDX_HEADS = 8
IDX_DIM = 64
IDX_TOPK_MAX = 256
B_HEADS = 4
B_QK_DIM = 64
B_V_DIM = 128
B_QK_WIDTH = B_HEADS * B_QK_DIM
B_WIDTH = B_HEADS * B_V_DIM
REL_BUCKETS = 32
REL_MAX_DIST = 128
TOP_K = 4
SWIGLU_LIMIT = 7.0
SWIGLU_ALPHA = 1.702
ROPE_BASE = 10000.0
RMS_EPS = 1e-5
LN_EPS = 1e-6

LANES = 128
SUBLANES = 8
VMEM_LIMIT = 56 * 1024 * 1024

MXU_DTYPE = jnp.bfloat16

INT_MIN = -(2 ** 31)
INT_MAX = 2 ** 31 - 1
F32_MIN = float(np.finfo(np.float32).min)
LOG2E = math.log2(math.e)

PROJ_TM = 512
RET_C = 256
DSA_T = 256
MOE_TM = 512
MOE_C = 512
SC_WINDOW = 128
PACK_W = 256


def _dot(a, b):
    return jnp.dot(a, b, preferred_element_type=jnp.float32)


def _dot_nt(a, b):
    return lax.dot_general(a, b, (((1,), (1,)), ((), ())), preferred_element_type=jnp.float32)


def _rms(x, gain):
    return x * lax.rsqrt(jnp.mean(x * x, axis=-1, keepdims=True) + RMS_EPS) * gain


_SEG = {}
_off = 0
for _name, _w in (("qa", A_WIDTH), ("ka", A_WIDTH), ("va", A_WIDTH), ("qi", IDX_HEADS * IDX_DIM),
                  ("ki", LANES), ("wi", LANES),
                  ("qb", B_HEADS * LANES), ("qbr", B_HEADS * LANES),
                  ("kb", B_HEADS * LANES), ("kbr", B_HEADS * LANES),
                  ("vb", B_WIDTH), ("gb", B_WIDTH)):
    _SEG[_name] = (_off, _off + _w)
    _off += _w
PROJ_N = _off


def _proj_kernel(x_ref, g_ref, w_ref, cos_ref, sin_ref, lng_ref, lnb_ref,
                 qa_ref, ka_ref, va_ref, qi_ref, ki_ref, wi_ref, qb_ref, kb_ref, vb_ref, sg_ref):
    h = _rms(x_ref[...], g_ref[...]).astype(MXU_DTYPE)

    def seg(name):
        a, b = _SEG[name]
        return _dot(h, w_ref[:, a:b])

    qa_ref[...] = (seg("qa") * (A_HEAD_DIM ** -0.5 * LOG2E)).astype(qa_ref.dtype)
    ka_ref[...] = seg("ka").astype(ka_ref.dtype)
    va_ref[...] = seg("va").astype(va_ref.dtype)
    qi_ref[...] = seg("qi").astype(qi_ref.dtype)
    ki = seg("ki")[:, :IDX_DIM]
    mu = jnp.mean(ki, axis=-1, keepdims=True)
    var = jnp.mean(jnp.square(ki - mu), axis=-1, keepdims=True)
    ki = (ki - mu) * lax.rsqrt(var + LN_EPS) * lng_ref[...] + lnb_ref[...]
    ki_ref[...] = ki.astype(ki_ref.dtype)
    wi_ref[...] = seg("wi")[:, :IDX_HEADS] * ((IDX_HEADS ** -0.5) * (IDX_DIM ** -0.5))
    cos = cos_ref[...]
    sin = sin_ref[...]
    qb_ref[...] = (seg("qb") * cos + seg("qbr") * sin).astype(qb_ref.dtype)
    kb_ref[...] = ((seg("kb") * cos + seg("kbr") * sin) * (B_QK_DIM ** -0.5)).astype(kb_ref.dtype)
    vb_ref[...] = seg("vb").astype(vb_ref.dtype)
    g = seg("gb")
    sg_ref[...] = g / (1.0 + jnp.exp(-g))


def _pack_proj_weight(w_in):
    d = w_in.shape[0]
    sizes = (A_WIDTH, A_WIDTH, A_WIDTH, IDX_HEADS * IDX_DIM, IDX_DIM, IDX_HEADS,
             B_QK_WIDTH, B_QK_WIDTH, B_WIDTH, B_WIDTH)
    cuts = np.cumsum(sizes)[:-1].tolist()
    wqa, wka, wva, wqi, wki, wwi, wqb, wkb, wvb, wgb = jnp.split(w_in, cuts, axis=-1)

    def pad_cols(w, n):
        return jnp.pad(w, ((0, 0), (0, n - w.shape[1])))

    def head_slots(w):
        w = w.reshape(d, B_HEADS, B_QK_DIM)
        return jnp.pad(w, ((0, 0), (0, 0), (0, LANES - B_QK_DIM))).reshape(d, B_HEADS * LANES)

    def rot_half(w):
        w = w.reshape(d, B_HEADS, 2, B_QK_DIM // 2)
        return jnp.stack([-w[:, :, 1], w[:, :, 0]], axis=2).reshape(d, B_QK_WIDTH)

    parts = [wqa, wka, wva, wqi, pad_cols(wki, LANES), pad_cols(wwi, LANES),
             head_slots(wqb), head_slots(rot_half(wqb)), head_slots(wkb), head_slots(rot_half(wkb)),
             wvb, wgb]
    return jnp.concatenate(parts, axis=-1).astype(MXU_DTYPE)


def _rotary_tables(s):
    half = B_QK_DIM // 2
    inv = 1.0 / (ROPE_BASE ** jnp.linspace(0.0, 1.0, half, dtype=jnp.float32))
    ang = jnp.arange(s, dtype=jnp.int32).astype(jnp.float32)[:, None] * inv[None, :]

    def slots(t):
        t = jnp.concatenate([t, t, jnp.zeros((s, LANES - B_QK_DIM), jnp.float32)], axis=-1)
        return jnp.tile(t, (1, B_HEADS))

    return slots(jnp.cos(ang)), slots(jnp.sin(ang))


def _in_proj(x2, g1, wp, cos, sin, lng, lnb, s):
    t, d = x2.shape
    tm = min(PROJ_TM, s)
    nt = t // tm
    ns = s // tm
    row = lambda i: (i, 0)
    fixed = lambda i: (0, 0)
    pos = lambda i: (i % ns, 0)
    bw = B_HEADS * LANES
    out_shapes = [
        jax.ShapeDtypeStruct((t, A_WIDTH), MXU_DTYPE),
        jax.ShapeDtypeStruct((t, A_WIDTH), MXU_DTYPE),
        jax.ShapeDtypeStruct((t, A_WIDTH), MXU_DTYPE),
        jax.ShapeDtypeStruct((t, IDX_HEADS * IDX_DIM), MXU_DTYPE),
        jax.ShapeDtypeStruct((t, IDX_DIM), MXU_DTYPE),
        jax.ShapeDtypeStruct((t, IDX_HEADS), jnp.float32),
        jax.ShapeDtypeStruct((t, bw), MXU_DTYPE),
        jax.ShapeDtypeStruct((t, bw), MXU_DTYPE),
        jax.ShapeDtypeStruct((t, B_WIDTH), MXU_DTYPE),
        jax.ShapeDtypeStruct((t, B_WIDTH), jnp.float32),
    ]
    out_specs = [pl.BlockSpec((tm, o.shape[1]), row) for o in out_shapes]
    return pl.pallas_call(
        _proj_kernel,
        grid=(nt,),
        in_specs=[
            pl.BlockSpec((tm, d), row),
            pl.BlockSpec((1, d), fixed),
            pl.BlockSpec((d, PROJ_N), fixed),
            pl.BlockSpec((tm, bw), pos),
            pl.BlockSpec((tm, bw), pos),
            pl.BlockSpec((1, IDX_DIM), fixed),
            pl.BlockSpec((1, IDX_DIM), fixed),
        ],
        out_specs=out_specs,
        out_shape=out_shapes,
        compiler_params=pltpu.CompilerParams(dimension_semantics=("parallel",),
                                             vmem_limit_bytes=VMEM_LIMIT),
        name="in_proj",
    )(x2, g1, wp, cos, sin, lng, lnb)


def _ret_kernel(q_ref, k_ref, v_ref, sg_ref, gain_ref, decay_ref, zeta_ref, xi_ref, cd_ref,
                o_ref, state_ref):
    @pl.when(pl.program_id(1) == 0)
    def _():
        state_ref[...] = jnp.zeros_like(state_ref)

    for h in range(B_HEADS):
        sl = slice(h * LANES, (h + 1) * LANES)
        q = q_ref[0, :, sl]
        k = k_ref[0, :, sl]
        v = v_ref[0, :, sl]
        state = state_ref[h]
        scores = _dot_nt(q, k) * decay_ref[h]
        intra = _dot(scores.astype(MXU_DTYPE), v)
        cross = _dot(q, state.astype(MXU_DTYPE)) * xi_ref[h]
        y = intra + cross
        kz = (k.astype(jnp.float32) * zeta_ref[h]).T.astype(MXU_DTYPE)
        state_ref[h] = cd_ref[h] * state + _dot(kz, v)
        mu = jnp.mean(y, axis=-1, keepdims=True)
        var = jnp.mean(jnp.square(y - mu), axis=-1, keepdims=True)
        yn = (y - mu) * lax.rsqrt(var + LN_EPS) * gain_ref[:, sl]
        o_ref[0, :, sl] = (sg_ref[0, :, sl] * yn).astype(o_ref.dtype)


def _retention(qb, kb, vb, sg, gain, c):
    b, s, bw = qb.shape
    f32 = jnp.float32
    log_g = jnp.log(1.0 - 2.0 ** (-5.0 - jnp.arange(B_HEADS, dtype=f32)))
    n = jnp.arange(c, dtype=f32)
    diff = n[:, None] - n[None, :]
    decay = jnp.where(diff >= 0, jnp.exp(log_g[:, None, None] * jnp.maximum(diff, 0.0)), 0.0)
    zeta = jnp.exp(log_g[:, None] * (c - 1.0 - n)[None, :])[:, :, None]
    xi = jnp.exp(log_g[:, None] * (n + 1.0)[None, :])[:, :, None]
    cd = jnp.broadcast_to(jnp.exp(log_g * c)[:, None, None], (B_HEADS, 1, LANES))
    blk = lambda bi, ci: (bi, ci, 0)
    fix3 = lambda bi, ci: (0, 0, 0)
    return pl.pallas_call(
        _ret_kernel,
        grid=(b, s // c),
        in_specs=[
            pl.BlockSpec((1, c, bw), blk),
            pl.BlockSpec((1, c, bw), blk),
            pl.BlockSpec((1, c, B_WIDTH), blk),
            pl.BlockSpec((1, c, B_WIDTH), blk),
            pl.BlockSpec((1, B_WIDTH), lambda bi, ci: (0, 0)),
            pl.BlockSpec((B_HEADS, c, c), fix3),
            pl.BlockSpec((B_HEADS, c, 1), fix3),
            pl.BlockSpec((B_HEADS, c, 1), fix3),
            pl.BlockSpec((B_HEADS, 1, LANES), fix3),
        ],
        out_specs=pl.BlockSpec((1, c, B_WIDTH), blk),
        out_shape=jax.ShapeDtypeStruct((b, s, B_WIDTH), MXU_DTYPE),
        scratch_shapes=[pltpu.VMEM((B_HEADS, LANES, LANES), jnp.float32)],
        compiler_params=pltpu.CompilerParams(dimension_semantics=("parallel", "arbitrary"),
                                             vmem_limit_bytes=VMEM_LIMIT),
        name="retention",
    )(qb, kb, vb, sg, gain, decay, zeta, xi, cd)


def _t5_bucket(rel):
    nb = REL_BUCKETS // 2
    max_exact = nb // 2
    ret = jnp.where(rel > 0, nb, 0)
    n = jnp.abs(rel)
    nf = jnp.maximum(n, 1).astype(jnp.float32)
    large = max_exact + (jnp.log(nf / max_exact) / math.log(REL_MAX_DIST / max_exact)
                         * (nb - max_exact)).astype(jnp.int32)
    large = jnp.minimum(large, nb - 1)
    return ret + jnp.where(n < max_exact, n, large)


def _bias_tiles(rel_bias, t):
    i = jnp.arange(t, dtype=jnp.int32)
    rel_d = i[:, None] - i[None, :]
    rb = rel_bias.astype(jnp.float32)
    far = rb[_t5_bucket(jnp.int32(-(t + 1)))]

    def tile(rel):
        hot = jax.nn.one_hot(_t5_bucket(rel), REL_BUCKETS, dtype=jnp.float32)
        return (jnp.einsum("kqb,bh->hkq", hot, rb, precision=lax.Precision.HIGHEST) - far[:, None, None]) * LOG2E

    near = tile(rel_d - t)
    return jnp.stack([jnp.zeros_like(near), near, tile(rel_d)])


def _dsa_kernel(qat_ref, qit_ref, wit_ref, k_ref, vt_ref, ki_ref, bias_ref, o_ref,
                keys_ref, mb_ref, sbuf_ref, m_ref, al_ref, l_ref, acc_ref, *, topk):
    t = DSA_T
    i32 = jnp.int32
    qblk = pl.program_id(1)
    ntile = qblk + 1
    qcol = lax.broadcasted_iota(i32, (1, t), 1)
    lim_local = (qcol // CHUNK + 1) * CHUNK
    keep_all = (qblk * t + lim_local) <= topk
    krow = lax.broadcasted_iota(i32, (t, t), 0)

    def tile_off(j):
        return pl.multiple_of(j * t, t)

    qit = [qit_ref[0, h * IDX_DIM:(h + 1) * IDX_DIM, :] for h in range(IDX_HEADS)]
    wit = [wit_ref[0, h:h + 1, :] for h in range(IDX_HEADS)]

    def score_tile(j, diag):
        off = tile_off(j)
        kt = ki_ref[0, pl.ds(off, t), :]
        sc = jnp.zeros((t, t), jnp.float32)
        for h in range(IDX_HEADS):
            sc = sc + wit[h] * jnp.maximum(_dot(kt, qit[h]), 0.0)
        sc = jnp.where(sc == 0.0, 0.0, sc)
        bits = pltpu.bitcast(sc, i32)
        key = bits ^ ((bits >> 31) & INT_MAX)
        if diag:
            key = jnp.where(krow < lim_local, key, INT_MIN)
        keys_ref[pl.ds(off, t), :] = key

    def far_scores(j, c):
        score_tile(j, False)
        return c

    lax.fori_loop(0, qblk, far_scores, 0)
    score_tile(qblk, True)

    sub = lax.broadcasted_iota(i32, (SUBLANES, t), 0)

    def rows8(v):
        return jnp.broadcast_to(v, (SUBLANES, t))

    def count(pred):
        def body(j, cnt):
            off = tile_off(j)
            tile = keys_ref[pl.ds(off, t), :]
            for r in range(0, t, SUBLANES):
                cnt = cnt + jnp.where(pred(tile[r:r + SUBLANES], sub + (off + r)), 1, 0)
            return cnt
        cnt = lax.fori_loop(0, ntile, body, jnp.zeros((SUBLANES, t), i32))
        return jnp.sum(cnt, axis=0, keepdims=True)

    def unsettled(lo, hi):
        return jnp.max(jnp.where(hi != lo + 1, 1.0, 0.0))

    def bis_cond(c):
        it, active, _, _ = c
        return jnp.logical_and(it < 32, active > 0.0)

    def bis_pass(lo, hi):
        mid = (lo >> 1) + (hi >> 1) + (lo & hi & 1)
        mid8 = rows8(mid)
        n = count(lambda k, idx: k >= mid8)
        ge = n >= topk
        lo = jnp.where(ge, mid, lo)
        hi = jnp.where(n == topk, mid + 1, jnp.where(ge, hi, mid))
        return lo, hi

    def bis_body(c):
        it, _, lo, hi = c
        lo, hi = bis_pass(*bis_pass(lo, hi))
        return it + 2, unsettled(lo, hi), lo, hi

    lo0 = jnp.full((1, t), INT_MIN, i32)
    hi0 = jnp.where(keep_all, INT_MIN + 1, INT_MAX)
    _, _, lo, _ = lax.while_loop(bis_cond, bis_body, (jnp.int32(0), unsettled(lo0, hi0), lo0, hi0))
    tau = jnp.where(keep_all, INT_MIN + 1, lo)
    tau8 = rows8(tau)
    n_ge = count(lambda k, idx: k >= tau8)
    tie = jnp.logical_and(n_ge > topk, jnp.logical_not(keep_all))

    @pl.when(jnp.max(jnp.where(tie, 1, 0)) > 0)
    def _():
        n_eq_keep = topk - count(lambda k, idx: k > tau8)

        def bis_pos(_, carry):
            plo, phi = carry
            pmid = (plo + phi) >> 1
            pmid8 = rows8(pmid)
            ok = count(lambda k, idx: jnp.logical_and(k == tau8, idx < pmid8)) <= n_eq_keep
            return jnp.where(ok, pmid, plo), jnp.where(ok, phi, pmid)

        smax = pl.num_programs(1) * t
        nbits = int(math.ceil(math.log2(keys_ref.shape[0]))) + 1
        plo, _ = lax.fori_loop(0, nbits, bis_pos,
                               (jnp.zeros((1, t), i32), jnp.zeros((1, t), i32) + smax))
        cut = jnp.where(tie, plo, smax)

        def drop(j, c):
            off = tile_off(j)
            k = keys_ref[pl.ds(off, t), :]
            dead = jnp.logical_and(k == tau, (krow + j * t) >= cut)
            keys_ref[pl.ds(off, t), :] = jnp.where(dead, INT_MIN, k)
            return c

        lax.fori_loop(0, ntile, drop, 0)

    m_ref[...] = jnp.full(m_ref.shape, F32_MIN, jnp.float32)
    al_ref[...] = jnp.ones(al_ref.shape, jnp.float32)
    l_ref[...] = jnp.zeros(l_ref.shape, jnp.float32)
    acc_ref[...] = jnp.zeros(acc_ref.shape, jnp.float32)
    srow = lax.broadcasted_iota(i32, (LANES, t), 0)
    qmt = []
    for h in range(A_HEADS):
        pair = qat_ref[0, (h // 2) * LANES:(h // 2 + 1) * LANES, :]
        qmt.append(jnp.where((srow // A_HEAD_DIM) == (h % 2), pair, jnp.zeros_like(pair)))

    def bias_kind(j):
        return jnp.where(j >= qblk, 2, jnp.where(j >= qblk - 1, 1, 0))

    def set_mask(j):
        mb_ref[...] = jnp.where(keys_ref[pl.ds(tile_off(j), t), :] >= tau, 0.0, -jnp.inf)

    def consume(j, h):
        m_h = m_ref[h]
        a_h = al_ref[h]
        p = jnp.exp2(sbuf_ref[h] - m_h)
        l_ref[h] = a_h * l_ref[h] + jnp.sum(p, axis=0, keepdims=True)
        pv = _dot(vt_ref[0, h * A_HEAD_DIM:(h + 1) * A_HEAD_DIM, pl.ds(tile_off(j), t)], p.astype(MXU_DTYPE))
        acc_ref[h] = a_h * acc_ref[h] + pv

    def produce(j, h, kind):
        g = h // 2
        s = _dot(k_ref[0, pl.ds(tile_off(j), t), g * LANES:(g + 1) * LANES], qmt[h]) + mb_ref[...]
        if kind is not None:
            s = s + bias_ref[kind, h]
        m_h = m_ref[h]
        m_new = jnp.maximum(m_h, jnp.max(s, axis=0, keepdims=True))
        al_ref[h] = jnp.exp2(m_h - m_new)
        m_ref[h] = m_new
        sbuf_ref[h] = s

    set_mask(0)
    for h in range(A_HEADS):
        produce(0, h, bias_kind(0))

    def step(j, kind):
        set_mask(j + 1)
        for h in range(A_HEADS):
            consume(j, h)
            produce(j + 1, h, kind)

    def far_step(j, c):
        step(j, None)
        return c

    def near_step(j, c):
        step(j, bias_kind(j + 1))
        return c

    n_far = jnp.maximum(qblk - 2, 0)
    lax.fori_loop(0, n_far, far_step, 0)
    lax.fori_loop(n_far, qblk, near_step, 0)
    for h in range(A_HEADS):
        consume(qblk, h)

    out_t = jnp.concatenate([acc_ref[h] / l_ref[h] for h in range(A_HEADS)], axis=0)
    o_ref[0] = out_t.T.astype(o_ref.dtype)


def _dsa(qat, qit, wit, k, vt, ki, bias, topk):
    b, _, s = qat.shape
    t = DSA_T
    qcols = lambda bi, qi_: (bi, 0, qi_)
    per_b = lambda bi, qi_: (bi, 0, 0)
    once = pl.Buffered(1)
    return pl.pallas_call(
        functools.partial(_dsa_kernel, topk=topk),
        grid=(b, s // t),
        in_specs=[
            pl.BlockSpec((1, A_WIDTH, t), qcols),
            pl.BlockSpec((1, IDX_HEADS * IDX_DIM, t), qcols),
            pl.BlockSpec((1, IDX_HEADS, t), qcols),
            pl.BlockSpec((1, s, A_WIDTH), per_b, pipeline_mode=once),
            pl.BlockSpec((1, A_WIDTH, s), per_b, pipeline_mode=once),
            pl.BlockSpec((1, s, IDX_DIM), per_b, pipeline_mode=once),
            pl.BlockSpec((3, A_HEADS, t, t), lambda bi, qi_: (0, 0, 0, 0), pipeline_mode=once),
        ],
        out_specs=pl.BlockSpec((1, t, A_WIDTH), lambda bi, qi_: (bi, qi_, 0)),
        out_shape=jax.ShapeDtypeStruct((b, s, A_WIDTH), MXU_DTYPE),
        scratch_shapes=[
            pltpu.VMEM((s, t), jnp.int32),
            pltpu.VMEM((t, t), jnp.float32),
            pltpu.VMEM((A_HEADS, t, t), jnp.float32),
            pltpu.VMEM((A_HEADS, 1, t), jnp.float32),
            pltpu.VMEM((A_HEADS, 1, t), jnp.float32),
            pltpu.VMEM((A_HEADS, 1, t), jnp.float32),
            pltpu.VMEM((A_HEADS, A_HEAD_DIM, t), jnp.float32),
        ],
        compiler_params=pltpu.CompilerParams(dimension_semantics=("parallel", "arbitrary"),
                                             vmem_limit_bytes=VMEM_LIMIT),
        name="dsa",
    )(qat, qit, wit, k, vt, ki, bias)


HI16 = -65536


def _pack(v):
    bits = pltpu.bitcast(v, jnp.int32)
    n = v.shape[1] // 2
    return lax.shift_right_logical(bits[:, :n], 16) | (bits[:, n:] & HI16)


def _unpack(wa, wb):
    w = jnp.concatenate([wa, wb], axis=1)
    lo = pltpu.bitcast(w << 16, jnp.float32)
    hi = pltpu.bitcast(w & HI16, jnp.float32)
    return jnp.concatenate([lo, hi], axis=1)


def _outproj_kernel(x_ref, oa_ref, ob_ref, wa_ref, wb_ref, g2_ref, wr_ref, br_ref, tri_ref,
                    x1_ref, ha_ref, hb_ref, route_ref, gate_ref, cnt_ref):
    x1 = x_ref[...] + _dot(oa_ref[...], wa_ref[...]) + _dot(ob_ref[...], wb_ref[...])
    x1_ref[...] = x1
    h2 = _rms(x1, g2_ref[...])
    h2m = h2.astype(MXU_DTYPE)
    hw = _pack(h2m.astype(jnp.float32))
    ha_ref[...] = hw[:, :PACK_W]
    hb_ref[...] = hw[:, PACK_W:]
    logits = _dot(h2m, wr_ref[...]) + br_ref[...]
    lane = lax.broadcasted_iota(jnp.int32, logits.shape, 1)
    work = logits
    vals, hots, ids = [], [], []
    for _ in range(TOP_K):
        mx = jnp.max(work, axis=1, keepdims=True)
        idx = jnp.min(jnp.where(work == mx, lane, LANES), axis=1, keepdims=True)
        hot = lane == idx
        vals.append(mx)
        hots.append(hot)
        ids.append(idx)
        work = jnp.where(hot, -jnp.inf, work)
    es = [jnp.exp(v - vals[0]) for v in vals]
    denom = es[0]
    for e in es[1:]:
        denom = denom + e
    sel = jnp.zeros(logits.shape, jnp.float32)
    for hot in hots:
        sel = sel + jnp.where(hot, 1.0, 0.0)
    rank = _dot(tri_ref[...], sel.astype(MXU_DTYPE))
    route = jnp.zeros(logits.shape, jnp.int32)
    gate = jnp.zeros(logits.shape, jnp.float32)
    for k in range(TOP_K):
        r_k = jnp.sum(jnp.where(hots[k], rank, 0.0), axis=1, keepdims=True).astype(jnp.int32)
        route = route + jnp.where(lane == k, ids[k], 0) + jnp.where(lane == TOP_K + k, r_k, 0)
        gate = gate + jnp.where(lane == k, es[k] / denom, 0.0)
    route_ref[...] = route
    gate_ref[...] = gate
    cnt = jnp.sum(sel, axis=0, keepdims=True).astype(jnp.int32)
    cnt_ref[0] = jnp.broadcast_to(cnt, (SUBLANES, LANES))


def _out_proj(x2, oa, ob, wa, wb, g2, wr, br):
    t, d = x2.shape
    tm = min(PROJ_TM, t)
    row = lambda i: (i, 0)
    fixed = lambda i: (0, 0)
    tri = (jnp.arange(tm)[:, None] > jnp.arange(tm)[None, :]).astype(MXU_DTYPE)
    return pl.pallas_call(
        _outproj_kernel,
        grid=(t // tm,),
        in_specs=[
            pl.BlockSpec((tm, d), row),
            pl.BlockSpec((tm, A_WIDTH), row),
            pl.BlockSpec((tm, B_WIDTH), row),
            pl.BlockSpec((A_WIDTH, d), fixed),
            pl.BlockSpec((B_WIDTH, d), fixed),
            pl.BlockSpec((1, d), fixed),
            pl.BlockSpec((d, LANES), fixed),
            pl.BlockSpec((1, LANES), fixed),
            pl.BlockSpec((tm, tm), fixed),
        ],
        out_specs=[pl.BlockSpec((tm, d), row), pl.BlockSpec((tm, PACK_W), row), pl.BlockSpec((tm, PACK_W), row),
                   pl.BlockSpec((tm, LANES), row), pl.BlockSpec((tm, LANES), row),
                   pl.BlockSpec((1, SUBLANES, LANES), lambda i: (i, 0, 0))],
        out_shape=[jax.ShapeDtypeStruct((t, d), jnp.float32),
                   jax.ShapeDtypeStruct((t, PACK_W), jnp.int32),
                   jax.ShapeDtypeStruct((t, PACK_W), jnp.int32),
                   jax.ShapeDtypeStruct((t, LANES), jnp.int32),
                   jax.ShapeDtypeStruct((t, LANES), jnp.float32),
                   jax.ShapeDtypeStruct((t // tm, SUBLANES, LANES), jnp.int32)],
        compiler_params=pltpu.CompilerParams(dimension_semantics=("parallel",),
                                             vmem_limit_bytes=VMEM_LIMIT),
        name="out_proj",
    )(x2, oa, ob, wa, wb, g2, wr, br, tri)


def _sc_mesh():
    return plsc.VectorSubcoreMesh(core_axis_name="core", subcore_axis_name="subcore")


def _sc_scatter_rows(x, idx, n_rows):
    t, w = x.shape
    nk = idx.shape[0]

    @pl.kernel(out_type=jax.ShapeDtypeStruct((n_rows, w), x.dtype), mesh=_sc_mesh(), scratch_types=[])
    def scatter(x_hbm, i_hbm, o_hbm):
        def body(x_vmem, i_vmem):
            pltpu.sync_copy(x_vmem, o_hbm.at[i_vmem.at[0]])

        pltpu.emit_pipeline(
            body, grid=(t // SC_WINDOW, nk),
            in_specs=[pl.BlockSpec((SC_WINDOW, w), lambda i, k: (i, 0)),
                      pl.BlockSpec((1, SC_WINDOW), lambda i, k: (k, i))],
            out_specs=[],
            core_axis_name=("core", "subcore"),
            dimension_semantics=(pltpu.PARALLEL, pltpu.ARBITRARY),
        )(x_hbm, i_hbm)

    return scatter(x, idx)


def _sc_gather_rows(x, idx):
    n = idx.shape[0]
    w = x.shape[1]

    @pl.kernel(out_type=jax.ShapeDtypeStruct((n, w), x.dtype), mesh=_sc_mesh())
    def gather(x_hbm, i_hbm, o_hbm):
        def body(i_vmem, o_vmem):
            pltpu.sync_copy(x_hbm.at[i_vmem.at[0]], o_vmem)

        pltpu.emit_pipeline(
            body, grid=(n // SC_WINDOW,),
            in_specs=[pl.BlockSpec((1, SC_WINDOW), lambda i: (0, i))],
            out_specs=[pl.BlockSpec((SC_WINDOW, w), lambda i: (i, 0))],
            core_axis_name=("core", "subcore"),
            dimension_semantics=(pltpu.PARALLEL,),
        )(i_hbm, o_hbm)

    return gather(x, idx.reshape(1, n))


def _route_plan(cnt, route, ne, nch):
    c = MOE_C
    tiles = cnt.shape[0]
    tm = route.shape[0] // tiles
    tot = jnp.sum(cnt, axis=0)
    nchunk = (tot + c - 1) // c
    cstart = jnp.cumsum(nchunk) - nchunk
    tile_base = (cstart * c)[None, :] + jnp.cumsum(cnt, axis=0) - cnt
    eid = route[:, :TOP_K]
    rank = route[:, TOP_K:2 * TOP_K]
    base_tok = jnp.repeat(tile_base, tm, axis=0)
    hot = eid[:, :, None] == jnp.arange(LANES, dtype=jnp.int32)[None, None, :]
    pos = jnp.sum(jnp.where(hot, base_tok[:, None, :], 0), axis=-1) + rank
    ci = jnp.arange(nch, dtype=jnp.int32)
    cend = jnp.cumsum(nchunk)[:ne]
    ce = jnp.minimum(jnp.sum(ci[:, None] >= cend[None, :], axis=1), ne - 1).astype(jnp.int32)
    cn = jnp.clip(tot[ce] - (ci - cstart[ce]) * c, 0, c)
    cn = jnp.where(ci < cend[ne - 1], cn, 0).astype(jnp.int32)
    return pos.T, ce, cn


def _expert_kernel(ce_ref, cn_ref, xa_ref, xb_ref, w1_ref, b1_ref, w2_ref, b2_ref, ya_ref, yb_ref):
    n = cn_ref[pl.program_id(0)]

    @pl.when(n == 0)
    def _():
        ya_ref[...] = jnp.zeros_like(ya_ref)
        yb_ref[...] = jnp.zeros_like(yb_ref)

    @pl.when(n > 0)
    def _():
        _expert_chunk(n, xa_ref, xb_ref, w1_ref, b1_ref, w2_ref, b2_ref, ya_ref, yb_ref)


def _expert_chunk(n, xa_ref, xb_ref, w1_ref, b1_ref, w2_ref, b2_ref, ya_ref, yb_ref):
    rows = xa_ref.shape[0]
    row = lax.broadcasted_iota(jnp.int32, (rows, 1), 0)
    h2 = jnp.where(row < n, _unpack(xa_ref[...], xb_ref[...]), 0.0).astype(MXU_DTYPE)
    even = (lax.broadcasted_iota(jnp.int32, (rows, LANES), 1) % 2) == 0
    ff = w2_ref.shape[1]
    z = _dot(h2, w1_ref[0]) + b1_ref[0]
    acts = []
    for m in range(ff // LANES):
        za = z[:, 2 * m * LANES:(2 * m + 1) * LANES]
        zb = z[:, (2 * m + 1) * LANES:(2 * m + 2) * LANES]
        glu = jnp.where(even, za, pltpu.roll(zb, 1, 1))
        lin = jnp.where(even, pltpu.roll(za, LANES - 1, 1), zb)
        glu = jnp.minimum(glu, SWIGLU_LIMIT)
        lin = jnp.clip(lin, -SWIGLU_LIMIT, SWIGLU_LIMIT)
        act = glu * (1.0 / (1.0 + jnp.exp(-SWIGLU_ALPHA * glu))) * (lin + 1.0)
        acts.append(act.astype(MXU_DTYPE))
    y = _dot(jnp.concatenate(acts, axis=1), w2_ref[0]) + b2_ref[0]
    yw = _pack(y.astype(MXU_DTYPE).astype(jnp.float32))
    ya_ref[...] = yw[:, :PACK_W]
    yb_ref[...] = yw[:, PACK_W:]


def _pair_rows(w2):
    ne, ff, d = w2.shape
    half = LANES // 2
    return w2.reshape(ne, ff // LANES, 2, half, d).swapaxes(2, 3).reshape(ne, ff, d)


def _experts(ce, cn, xa, xb, w1, b1, w2, b2):
    nch = ce.shape[0]
    ne, d, f2 = w1.shape
    ff = f2 // 2
    c = MOE_C
    rows = lambda i, ce_, cn_: (i, 0)
    of_expert = lambda i, ce_, cn_: (ce_[i], 0, 0)
    out = jax.ShapeDtypeStruct((nch * c, PACK_W), jnp.int32)
    return pl.pallas_call(
        _expert_kernel,
        grid_spec=pltpu.PrefetchScalarGridSpec(
            num_scalar_prefetch=2,
            grid=(nch,),
            in_specs=[
                pl.BlockSpec((c, PACK_W), rows),
                pl.BlockSpec((c, PACK_W), rows),
                pl.BlockSpec((1, d, f2), of_expert),
                pl.BlockSpec((1, 1, f2), of_expert),
                pl.BlockSpec((1, ff, d), of_expert),
                pl.BlockSpec((1, 1, d), of_expert),
            ],
            out_specs=[pl.BlockSpec((c, PACK_W), rows), pl.BlockSpec((c, PACK_W), rows)],
        ),
        out_shape=[out, out],
        compiler_params=pltpu.CompilerParams(dimension_semantics=("arbitrary",),
                                             vmem_limit_bytes=VMEM_LIMIT),
        name="experts",
    )(ce, cn, xa, xb, w1, b1, w2, b2)


def _combine_kernel(x1_ref, gate_ref, ya_ref, yb_ref, gf_ref, o_ref):
    acc = x1_ref[...]
    gate = gate_ref[...]
    for k in range(TOP_K):
        acc = acc + gate[:, k:k + 1] * _unpack(ya_ref[k], yb_ref[k])
    o_ref[...] = _rms(acc, gf_ref[...])


def _combine(x1, gate, ya, yb, gf):
    t, d = x1.shape
    tm = min(MOE_TM, t)
    row = lambda i: (i, 0)
    picks = lambda i: (0, i, 0)
    return pl.pallas_call(
        _combine_kernel,
        grid=(t // tm,),
        in_specs=[
            pl.BlockSpec((tm, d), row),
            pl.BlockSpec((tm, LANES), row),
            pl.BlockSpec((TOP_K, tm, PACK_W), picks),
            pl.BlockSpec((TOP_K, tm, PACK_W), picks),
            pl.BlockSpec((1, d), lambda i: (0, 0)),
        ],
        out_specs=pl.BlockSpec((tm, d), row),
        out_shape=jax.ShapeDtypeStruct((t, d), jnp.float32),
        compiler_params=pltpu.CompilerParams(dimension_semantics=("parallel",),
                                             vmem_limit_bytes=VMEM_LIMIT),
        name="combine",
    )(x1, gate, ya, yb, gf)


def kernel(x, norm1_gain, w_in, rel_bias, idx_k_ln_gain, idx_k_ln_bias, ret_gn_gain,
           w_out, norm2_gain, w_router, b_router, w_mlp1, b_mlp1, w_mlp2, b_mlp2, final_gain):
    bsz, s, d = x.shape
    depth = w_in.shape[0]
    topk = min(IDX_TOPK_MAX, s // 4)
    assert depth == 1, "the final RMSNorm is fused into the expert kernel of the single layer"
    assert s % DSA_T == 0 and s % RET_C == 0 and topk <= DSA_T and DSA_T % CHUNK == 0
    assert s % min(PROJ_TM, s) == 0 and (bsz * s) % MOE_TM == 0
    ne = w_router.shape[-1]
    assert ne <= LANES and d == 4 * PACK_W
    assert (bsz * s) % SC_WINDOW == 0 and (bsz * s * TOP_K) % MOE_C == 0
    f32 = jnp.float32
    x2 = x.reshape(bsz * s, d).astype(f32)
    cos, sin = _rotary_tables(s)
    bias = _bias_tiles(rel_bias, DSA_T)

    for layer in range(depth):
        wp = _pack_proj_weight(w_in[layer])
        qa, ka, va, qi, ki, wi, qb, kb, vb, sg = _in_proj(
            x2, norm1_gain[layer][None, :].astype(f32), wp, cos, sin,
            idx_k_ln_gain[layer][None, :].astype(f32), idx_k_ln_bias[layer][None, :].astype(f32), s)

        r3 = lambda a: a.reshape(bsz, s, a.shape[-1])
        out_b = _retention(r3(qb), r3(kb), r3(vb), r3(sg),
                           ret_gn_gain[layer][None, :].astype(f32), min(RET_C, s))
        tr = lambda a: jnp.swapaxes(r3(a), 1, 2)
        out_a = _dsa(tr(qa), tr(qi), tr(wi), r3(ka), tr(va), r3(ki), bias, topk)

        wo = w_out[layer].astype(MXU_DTYPE)
        wr = jnp.pad(w_router[layer], ((0, 0), (0, LANES - ne))).astype(MXU_DTYPE)
        br = jnp.pad(b_router[layer].astype(f32), (0, LANES - ne), constant_values=-jnp.inf)[None, :]
        x1, ha, hb, route, gate, cnt = _out_proj(
            x2, out_a.reshape(bsz * s, A_WIDTH), out_b.reshape(bsz * s, B_WIDTH),
            wo[:A_WIDTH], wo[A_WIDTH:], norm2_gain[layer][None, :].astype(f32), wr, br)

        nch = (bsz * s * TOP_K) // MOE_C + ne
        pos, ce, cn = _route_plan(cnt[:, 0, :], route, ne, nch)
        xa = _sc_scatter_rows(ha, pos, nch * MOE_C)
        xb = _sc_scatter_rows(hb, pos, nch * MOE_C)
        ya, yb = _experts(ce, cn, xa, xb, w_mlp1[layer].astype(MXU_DTYPE), b_mlp1[layer].astype(f32)[:, None, :],
                          _pair_rows(w_mlp2[layer]).astype(MXU_DTYPE), b_mlp2[layer].astype(f32)[:, None, :])
        picks = pos.reshape(-1)
        ga = _sc_gather_rows(ya, picks).reshape(TOP_K, bsz * s, PACK_W)
        gb = _sc_gather_rows(yb, picks).reshape(TOP_K, bsz * s, PACK_W)
        x2 = _combine(x1, gate, ga, gb, final_gain[None, :].astype(f32))

    return x2.reshape(bsz, s, d).astype(x.dtype)
```

```python
import functools
import math

import numpy as np
import jax
import jax.numpy as jnp
from jax import lax
from jax.experimental import pallas as pl
from jax.experimental.pallas import tpu as pltpu
from jax.experimental.pallas import tpu_sc as plsc

CHUNK = 64
A_HEADS = 8
A_HEAD_DIM = 64
A_WIDTH = A_HEADS * A_HEAD_DIM
IDX_HEADS = 8
IDX_DIM = 64
IDX_TOPK_MAX = 256
B_HEADS = 4
B_QK_DIM = 64
B_V_DIM = 128
B_QK_WIDTH = B_HEADS * B_QK_DIM
B_WIDTH = B_HEADS * B_V_DIM
REL_BUCKETS = 32
REL_MAX_DIST = 128
TOP_K = 4
SWIGLU_LIMIT = 7.0
SWIGLU_ALPHA = 1.702
ROPE_BASE = 10000.0
RMS_EPS = 1e-5
LN_EPS = 1e-6

LANES = 128
SUBLANES = 8
VMEM_LIMIT = 56 * 1024 * 1024

MXU_DTYPE = jnp.bfloat16

INT_MIN = -(2 ** 31)
INT_MAX = 2 ** 31 - 1
F32_MIN = float(np.finfo(np.float32).min)
LOG2E = math.log2(math.e)

PROJ_TM = 512
RET_C = 256
DSA_T = 256
COUNT_LANES = 4
V_ROWS = A_HEAD_DIM + 16
MOE_TM = 512
MOE_C = 512
SC_WINDOW = 128
PACK_W = 256


def _dot(a, b):
    return jnp.dot(a, b, preferred_element_type=jnp.float32)


def _dot_nt(a, b):
    return lax.dot_general(a, b, (((1,), (1,)), ((), ())), preferred_element_type=jnp.float32)


def _rms(x, gain):
    return x * lax.rsqrt(jnp.mean(x * x, axis=-1, keepdims=True) + RMS_EPS) * gain


_SEG = {}
_off = 0
for _name, _w in (("qa", A_WIDTH), ("ka", A_WIDTH), ("va", A_WIDTH), ("qi", IDX_HEADS * IDX_DIM),
                  ("ki", LANES), ("wi", LANES),
                  ("qb", B_HEADS * LANES), ("qbr", B_HEADS * LANES),
                  ("kb", B_HEADS * LANES), ("kbr", B_HEADS * LANES),
                  ("vb", B_WIDTH), ("gb", B_WIDTH)):
    _SEG[_name] = (_off, _off + _w)
    _off += _w
PROJ_N = _off


def _proj_kernel(x_ref, g_ref, w_ref, cos_ref, sin_ref, lng_ref, lnb_ref,
                 qa_ref, ka_ref, va_ref, qi_ref, ki_ref, wi_ref, qb_ref, kb_ref, vb_ref, sg_ref):
    h = _rms(x_ref[...], g_ref[...]).astype(MXU_DTYPE)

    def seg(name):
        a, b = _SEG[name]
        return _dot(h, w_ref[:, a:b])

    qa_ref[...] = (seg("qa") * (A_HEAD_DIM ** -0.5 * LOG2E)).astype(qa_ref.dtype)
    ka_ref[...] = seg("ka").astype(ka_ref.dtype)
    va_ref[...] = seg("va").astype(va_ref.dtype)
    qi_ref[...] = seg("qi").astype(qi_ref.dtype)
    ki = seg("ki")[:, :IDX_DIM]
    mu = jnp.mean(ki, axis=-1, keepdims=True)
    var = jnp.mean(jnp.square(ki - mu), axis=-1, keepdims=True)
    ki = (ki - mu) * lax.rsqrt(var + LN_EPS) * lng_ref[...] + lnb_ref[...]
    ki_ref[...] = ki.astype(ki_ref.dtype)
    wi_ref[...] = seg("wi")[:, :IDX_HEADS] * ((IDX_HEADS ** -0.5) * (IDX_DIM ** -0.5))
    cos = cos_ref[...]
    sin = sin_ref[...]
    qb_ref[...] = (seg("qb") * cos + seg("qbr") * sin).astype(qb_ref.dtype)
    kb_ref[...] = ((seg("kb") * cos + seg("kbr") * sin) * (B_QK_DIM ** -0.5)).astype(kb_ref.dtype)
    vb_ref[...] = seg("vb").astype(vb_ref.dtype)
    g = seg("gb")
    sg_ref[...] = g / (1.0 + jnp.exp(-g))


def _pack_proj_weight(w_in):
    d = w_in.shape[0]
    sizes = (A_WIDTH, A_WIDTH, A_WIDTH, IDX_HEADS * IDX_DIM, IDX_DIM, IDX_HEADS,
             B_QK_WIDTH, B_QK_WIDTH, B_WIDTH, B_WIDTH)
    cuts = np.cumsum(sizes)[:-1].tolist()
    wqa, wka, wva, wqi, wki, wwi, wqb, wkb, wvb, wgb = jnp.split(w_in, cuts, axis=-1)

    def pad_cols(w, n):
        return jnp.pad(w, ((0, 0), (0, n - w.shape[1])))

    def head_slots(w):
        w = w.reshape(d, B_HEADS, B_QK_DIM)
        return jnp.pad(w, ((0, 0), (0, 0), (0, LANES - B_QK_DIM))).reshape(d, B_HEADS * LANES)

    def rot_half(w):
        w = w.reshape(d, B_HEADS, 2, B_QK_DIM // 2)
        return jnp.stack([-w[:, :, 1], w[:, :, 0]], axis=2).reshape(d, B_QK_WIDTH)

    parts = [wqa, wka, wva, wqi, pad_cols(wki, LANES), pad_cols(wwi, LANES),
             head_slots(wqb), head_slots(rot_half(wqb)), head_slots(wkb), head_slots(rot_half(wkb)),
             wvb, wgb]
    return jnp.concatenate(parts, axis=-1).astype(MXU_DTYPE)


def _rotary_tables(s):
    half = B_QK_DIM // 2
    inv = 1.0 / (ROPE_BASE ** jnp.linspace(0.0, 1.0, half, dtype=jnp.float32))
    ang = jnp.arange(s, dtype=jnp.int32).astype(jnp.float32)[:, None] * inv[None, :]

    def slots(t):
        t = jnp.concatenate([t, t, jnp.zeros((s, LANES - B_QK_DIM), jnp.float32)], axis=-1)
        return jnp.tile(t, (1, B_HEADS))

    return slots(jnp.cos(ang)), slots(jnp.sin(ang))


def _in_proj(x2, g1, wp, cos, sin, lng, lnb, s):
    t, d = x2.shape
    tm = min(PROJ_TM, s)
    nt = t // tm
    ns = s // tm
    row = lambda i: (i, 0)
    fixed = lambda i: (0, 0)
    pos = lambda i: (i % ns, 0)
    bw = B_HEADS * LANES
    out_shapes = [
        jax.ShapeDtypeStruct((t, A_WIDTH), MXU_DTYPE),
        jax.ShapeDtypeStruct((t, A_WIDTH), MXU_DTYPE),
        jax.ShapeDtypeStruct((t, A_WIDTH), MXU_DTYPE),
        jax.ShapeDtypeStruct((t, IDX_HEADS * IDX_DIM), MXU_DTYPE),
        jax.ShapeDtypeStruct((t, IDX_DIM), MXU_DTYPE),
        jax.ShapeDtypeStruct((t, IDX_HEADS), jnp.float32),
        jax.ShapeDtypeStruct((t, bw), MXU_DTYPE),
        jax.ShapeDtypeStruct((t, bw), MXU_DTYPE),
        jax.ShapeDtypeStruct((t, B_WIDTH), MXU_DTYPE),
        jax.ShapeDtypeStruct((t, B_WIDTH), jnp.float32),
    ]
    out_specs = [pl.BlockSpec((tm, o.shape[1]), row) for o in out_shapes]
    return pl.pallas_call(
        _proj_kernel,
        grid=(nt,),
        in_specs=[
            pl.BlockSpec((tm, d), row),
            pl.BlockSpec((1, d), fixed),
            pl.BlockSpec((d, PROJ_N), fixed),
            pl.BlockSpec((tm, bw), pos),
            pl.BlockSpec((tm, bw), pos),
            pl.BlockSpec((1, IDX_DIM), fixed),
            pl.BlockSpec((1, IDX_DIM), fixed),
        ],
        out_specs=out_specs,
        out_shape=out_shapes,
        compiler_params=pltpu.CompilerParams(dimension_semantics=("parallel",),
                                             vmem_limit_bytes=VMEM_LIMIT),
        name="in_proj",
    )(x2, g1, wp, cos, sin, lng, lnb)


def _ret_kernel(q_ref, k_ref, v_ref, sg_ref, gain_ref, decay_ref, zeta_ref, xi_ref, cd_ref,
                o_ref, state_ref):
    @pl.when(pl.program_id(1) == 0)
    def _():
        state_ref[...] = jnp.zeros_like(state_ref)

    for h in range(B_HEADS):
        sl = slice(h * LANES, (h + 1) * LANES)
        q = q_ref[0, :, sl]
        k = k_ref[0, :, sl]
        v = v_ref[0, :, sl]
        state = state_ref[h]
        scores = _dot_nt(q, k) * decay_ref[h]
        intra = _dot(scores.astype(MXU_DTYPE), v)
        cross = _dot(q, state.astype(MXU_DTYPE)) * xi_ref[h]
        y = intra + cross
        kz = (k.astype(jnp.float32) * zeta_ref[h]).T.astype(MXU_DTYPE)
        state_ref[h] = cd_ref[h] * state + _dot(kz, v)
        mu = jnp.mean(y, axis=-1, keepdims=True)
        var = jnp.mean(jnp.square(y - mu), axis=-1, keepdims=True)
        yn = (y - mu) * lax.rsqrt(var + LN_EPS) * gain_ref[:, sl]
        o_ref[0, :, sl] = (sg_ref[0, :, sl] * yn).astype(o_ref.dtype)


def _retention(qb, kb, vb, sg, gain, c):
    b, s, bw = qb.shape
    f32 = jnp.float32
    log_g = jnp.log(1.0 - 2.0 ** (-5.0 - jnp.arange(B_HEADS, dtype=f32)))
    n = jnp.arange(c, dtype=f32)
    diff = n[:, None] - n[None, :]
    decay = jnp.where(diff >= 0, jnp.exp(log_g[:, None, None] * jnp.maximum(diff, 0.0)), 0.0)
    zeta = jnp.exp(log_g[:, None] * (c - 1.0 - n)[None, :])[:, :, None]
    xi = jnp.exp(log_g[:, None] * (n + 1.0)[None, :])[:, :, None]
    cd = jnp.broadcast_to(jnp.exp(log_g * c)[:, None, None], (B_HEADS, 1, LANES))
    blk = lambda bi, ci: (bi, ci, 0)
    fix3 = lambda bi, ci: (0, 0, 0)
    return pl.pallas_call(
        _ret_kernel,
        grid=(b, s // c),
        in_specs=[
            pl.BlockSpec((1, c, bw), blk),
            pl.BlockSpec((1, c, bw), blk),
            pl.BlockSpec((1, c, B_WIDTH), blk),
            pl.BlockSpec((1, c, B_WIDTH), blk),
            pl.BlockSpec((1, B_WIDTH), lambda bi, ci: (0, 0)),
            pl.BlockSpec((B_HEADS, c, c), fix3),
            pl.BlockSpec((B_HEADS, c, 1), fix3),
            pl.BlockSpec((B_HEADS, c, 1), fix3),
            pl.BlockSpec((B_HEADS, 1, LANES), fix3),
        ],
        out_specs=pl.BlockSpec((1, c, B_WIDTH), blk),
        out_shape=jax.ShapeDtypeStruct((b, s, B_WIDTH), MXU_DTYPE),
        scratch_shapes=[pltpu.VMEM((B_HEADS, LANES, LANES), jnp.float32)],
        compiler_params=pltpu.CompilerParams(dimension_semantics=("parallel", "arbitrary"),
                                             vmem_limit_bytes=VMEM_LIMIT),
        name="retention",
    )(qb, kb, vb, sg, gain, decay, zeta, xi, cd)


def _t5_bucket(rel):
    nb = REL_BUCKETS // 2
    max_exact = nb // 2
    ret = jnp.where(rel > 0, nb, 0)
    n = jnp.abs(rel)
    nf = jnp.maximum(n, 1).astype(jnp.float32)
    large = max_exact + (jnp.log(nf / max_exact) / math.log(REL_MAX_DIST / max_exact)
                         * (nb - max_exact)).astype(jnp.int32)
    large = jnp.minimum(large, nb - 1)
    return ret + jnp.where(n < max_exact, n, large)


def _bias_tiles(rel_bias, t):
    i = jnp.arange(t, dtype=jnp.int32)
    rel_d = i[:, None] - i[None, :]
    rb = rel_bias.astype(jnp.float32)
    far = rb[_t5_bucket(jnp.int32(-(t + 1)))]

    def tile(rel):
        hot = jax.nn.one_hot(_t5_bucket(rel), REL_BUCKETS, dtype=jnp.float32)
        return (jnp.einsum("kqb,bh->hkq", hot, rb, precision=lax.Precision.HIGHEST) - far[:, None, None]) * LOG2E

    near = tile(rel_d - t)
    return jnp.stack([jnp.zeros_like(near), near, tile(rel_d)])


def _dsa_kernel(qat_ref, qit_ref, wit_ref, k_ref, vt_ref, ki_ref, bias_ref, o_ref,
                keys_ref, gmax_ref, mb_ref, sbuf_ref, m_ref, al_ref, acc_ref, *, topk):
    t = DSA_T
    i32 = jnp.int32
    qblk = pl.program_id(1)
    ntile = qblk + 1
    qcol = lax.broadcasted_iota(i32, (1, t), 1)
    lim_local = (qcol // CHUNK + 1) * CHUNK
    keep_all = (qblk * t + lim_local) <= topk
    krow = lax.broadcasted_iota(i32, (t, t), 0)

    def tile_off(j):
        return pl.multiple_of(j * t, t)

    qit = [qit_ref[0, h * IDX_DIM:(h + 1) * IDX_DIM, :] for h in range(IDX_HEADS)]
    wit = [wit_ref[0, h:h + 1, :] for h in range(IDX_HEADS)]

    def score_tile(j, diag):
        off = tile_off(j)
        kt = ki_ref[0, pl.ds(off, t), :]
        sc = jnp.zeros((t, t), jnp.float32)
        for h in range(IDX_HEADS):
            sc = sc + wit[h] * jnp.maximum(_dot(kt, qit[h]), 0.0)
        sc = jnp.where(sc == 0.0, 0.0, sc)
        bits = pltpu.bitcast(sc, i32)
        key = bits ^ ((bits >> 31) & INT_MAX)
        if diag:
            key = jnp.where(krow < lim_local, key, INT_MIN)
        keys_ref[pl.ds(off, t), :] = key
        gmax_ref[...] = jnp.maximum(gmax_ref[...], key)

    def far_scores(j, c):
        score_tile(j, False)
        return c

    gmax_ref[...] = jnp.full(gmax_ref.shape, INT_MIN, i32)
    lax.fori_loop(0, qblk, far_scores, 0)
    score_tile(qblk, True)

    sub = lax.broadcasted_iota(i32, (SUBLANES, t), 0)

    def rows8(v):
        return jnp.broadcast_to(v, (SUBLANES, t))

    def count(pred):
        def body(j, cnts):
            off = tile_off(j)
            tile = keys_ref[pl.ds(off, t), :]
            cnts = list(cnts)
            for i, r in enumerate(range(0, t, SUBLANES)):
                a = i % len(cnts)
                cnts[a] = cnts[a] + jnp.where(pred(tile[r:r + SUBLANES], sub + (off + r)), 1, 0)
            return tuple(cnts)
        zero = jnp.zeros((SUBLANES, t), i32)
        cnts = lax.fori_loop(0, ntile, body, (zero,) * COUNT_LANES)
        return jnp.sum(functools.reduce(lambda a, b: a + b, cnts), axis=0, keepdims=True)

    def unsettled(lo, hi):
        return jnp.max(jnp.where(hi != lo + 1, 1.0, 0.0))

    def bis_cond(c):
        it, active, _, _ = c
        return jnp.logical_and(it < 32, active > 0.0)

    def bis_pass(lo, hi):
        mid = (lo >> 1) + (hi >> 1) + (lo & hi & 1)
        mid8 = rows8(mid)
        n = count(lambda k, idx: k >= mid8)
        ge = n >= topk
        lo = jnp.where(ge, mid, lo)
        hi = jnp.where(n == topk, mid + 1, jnp.where(ge, hi, mid))
        return lo, hi

    def bis_body(c):
        it, _, lo, hi = c
        lo, hi = bis_pass(*bis_pass(lo, hi))
        return it + 2, unsettled(lo, hi), lo, hi

    gmax = gmax_ref[...]
    kmax = jnp.max(gmax, axis=0, keepdims=True)
    lo0 = jnp.where(keep_all, INT_MIN, jnp.min(gmax, axis=0, keepdims=True))
    hi0 = jnp.where(keep_all, INT_MIN + 1, jnp.where(kmax == INT_MAX, INT_MAX, kmax + 1))
    _, _, lo, _ = lax.while_loop(bis_cond, bis_body, (jnp.int32(0), unsettled(lo0, hi0), lo0, hi0))
    tau = jnp.where(keep_all, INT_MIN + 1, lo)
    tau8 = rows8(tau)
    n_ge = count(lambda k, idx: k >= tau8)
    tie = jnp.logical_and(n_ge > topk, jnp.logical_not(keep_all))

    @pl.when(jnp.max(jnp.where(tie, 1, 0)) > 0)
    def _():
        n_eq_keep = topk - count(lambda k, idx: k > tau8)

        def bis_pos(_, carry):
            plo, phi = carry
            pmid = (plo + phi) >> 1
            pmid8 = rows8(pmid)
            ok = count(lambda k, idx: jnp.logical_and(k == tau8, idx < pmid8)) <= n_eq_keep
            return jnp.where(ok, pmid, plo), jnp.where(ok, phi, pmid)

        smax = pl.num_programs(1) * t
        nbits = int(math.ceil(math.log2(keys_ref.shape[0]))) + 1
        plo, _ = lax.fori_loop(0, nbits, bis_pos,
                               (jnp.zeros((1, t), i32), jnp.zeros((1, t), i32) + smax))
        cut = jnp.where(tie, plo, smax)

        def drop(j, c):
            off = tile_off(j)
            k = keys_ref[pl.ds(off, t), :]
            dead = jnp.logical_and(k == tau, (krow + j * t) >= cut)
            keys_ref[pl.ds(off, t), :] = jnp.where(dead, INT_MIN, k)
            return c

        lax.fori_loop(0, ntile, drop, 0)

    m_ref[...] = jnp.full(m_ref.shape, F32_MIN, jnp.float32)
    al_ref[...] = jnp.ones(al_ref.shape, jnp.float32)
    acc_ref[...] = jnp.zeros(acc_ref.shape, jnp.float32)
    srow = lax.broadcasted_iota(i32, (LANES, t), 0)
    qmt = []
    for h in range(A_HEADS):
        pair = qat_ref[0, (h // 2) * LANES:(h // 2 + 1) * LANES, :]
        qmt.append(jnp.where((srow // A_HEAD_DIM) == (h % 2), pair, jnp.zeros_like(pair)))

    def bias_kind(j):
        return jnp.where(j >= qblk, 2, jnp.where(j >= qblk - 1, 1, 0))

    def set_mask(j):
        mb_ref[...] = jnp.where(keys_ref[pl.ds(tile_off(j), t), :] >= tau, 0.0, -jnp.inf)

    def consume(j, h):
        m_h = m_ref[h]
        a_h = al_ref[h]
        p = jnp.exp2(sbuf_ref[h] - m_h)
        pv = _dot(vt_ref[0, h * V_ROWS:(h + 1) * V_ROWS, pl.ds(tile_off(j), t)], p.astype(MXU_DTYPE))
        acc_ref[h] = a_h * acc_ref[h] + pv

    def produce(j, h, kind):
        g = h // 2
        s = _dot(k_ref[0, pl.ds(tile_off(j), t), g * LANES:(g + 1) * LANES], qmt[h]) + mb_ref[...]
        if kind is not None:
            s = s + bias_ref[kind, h]
        m_h = m_ref[h]
        m_new = jnp.maximum(m_h, jnp.max(s, axis=0, keepdims=True))
        al_ref[h] = jnp.exp2(m_h - m_new)
        m_ref[h] = m_new
        sbuf_ref[h] = s

    set_mask(0)
    for h in range(A_HEADS):
        produce(0, h, bias_kind(0))

    def step(j, kind):
        set_mask(j + 1)
        for h in range(A_HEADS):
            consume(j, h)
            produce(j + 1, h, kind)

    def far_step(j, c):
        step(j, None)
        return c

    def near_step(j, c):
        step(j, bias_kind(j + 1))
        return c

    n_far = jnp.maximum(qblk - 2, 0)
    lax.fori_loop(0, n_far, far_step, 0)
    lax.fori_loop(n_far, qblk, near_step, 0)
    for h in range(A_HEADS):
        consume(qblk, h)

    out_t = jnp.concatenate([acc_ref[h, :A_HEAD_DIM] / acc_ref[h, A_HEAD_DIM:A_HEAD_DIM + 1]
                             for h in range(A_HEADS)], axis=0)
    o_ref[0] = out_t.T.astype(o_ref.dtype)


def _dsa(qat, qit, wit, k, vt, ki, bias, topk):
    b, _, s = qat.shape
    t = DSA_T
    qcols = lambda bi, qi_: (bi, 0, qi_)
    per_b = lambda bi, qi_: (bi, 0, 0)
    once = pl.Buffered(1)
    return pl.pallas_call(
        functools.partial(_dsa_kernel, topk=topk),
        grid=(b, s // t),
        in_specs=[
            pl.BlockSpec((1, A_WIDTH, t), qcols),
            pl.BlockSpec((1, IDX_HEADS * IDX_DIM, t), qcols),
            pl.BlockSpec((1, IDX_HEADS, t), qcols),
            pl.BlockSpec((1, s, A_WIDTH), per_b, pipeline_mode=once),
            pl.BlockSpec((1, A_HEADS * V_ROWS, s), per_b, pipeline_mode=once),
            pl.BlockSpec((1, s, IDX_DIM), per_b, pipeline_mode=once),
            pl.BlockSpec((3, A_HEADS, t, t), lambda bi, qi_: (0, 0, 0, 0), pipeline_mode=once),
        ],
        out_specs=pl.BlockSpec((1, t, A_WIDTH), lambda bi, qi_: (bi, qi_, 0)),
        out_shape=jax.ShapeDtypeStruct((b, s, A_WIDTH), MXU_DTYPE),
        scratch_shapes=[
            pltpu.VMEM((s, t), jnp.int32),
            pltpu.VMEM((t, t), jnp.int32),
            pltpu.VMEM((t, t), jnp.float32),
            pltpu.VMEM((A_HEADS, t, t), jnp.float32),
            pltpu.VMEM((A_HEADS, 1, t), jnp.float32),
            pltpu.VMEM((A_HEADS, 1, t), jnp.float32),
            pltpu.VMEM((A_HEADS, V_ROWS, t), jnp.float32),
        ],
        compiler_params=pltpu.CompilerParams(dimension_semantics=("parallel", "arbitrary"),
                                             vmem_limit_bytes=VMEM_LIMIT),
        name="dsa",
    )(qat, qit, wit, k, vt, ki, bias)


HI16 = -65536


def _pack(v):
    bits = pltpu.bitcast(v, jnp.int32)
    n = v.shape[1] // 2
    return lax.shift_right_logical(bits[:, :n], 16) | (bits[:, n:] & HI16)


def _unpack(wa, wb):
    w = jnp.concatenate([wa, wb], axis=1)
    lo = pltpu.bitcast(w << 16, jnp.float32)
    hi = pltpu.bitcast(w & HI16, jnp.float32)
    return jnp.concatenate([lo, hi], axis=1)


def _outproj_kernel(x_ref, oa_ref, ob_ref, wa_ref, wb_ref, g2_ref, wr_ref, br_ref, tri_ref,
                    x1_ref, ha_ref, hb_ref, route_ref, gate_ref, cnt_ref):
    x1 = x_ref[...] + _dot(oa_ref[...], wa_ref[...]) + _dot(ob_ref[...], wb_ref[...])
    x1_ref[...] = x1
    h2 = _rms(x1, g2_ref[...])
    h2m = h2.astype(MXU_DTYPE)
    hw = _pack(h2m.astype(jnp.float32))
    ha_ref[...] = hw[:, :PACK_W]
    hb_ref[...] = hw[:, PACK_W:]
    logits = _dot(h2m, wr_ref[...]) + br_ref[...]
    lane = lax.broadcasted_iota(jnp.int32, logits.shape, 1)
    work = logits
    vals, hots, ids = [], [], []
    for _ in range(TOP_K):
        mx = jnp.max(work, axis=1, keepdims=True)
        idx = jnp.min(jnp.where(work == mx, lane, LANES), axis=1, keepdims=True)
        hot = lane == idx
        vals.append(mx)
        hots.append(hot)
        ids.append(idx)
        work = jnp.where(hot, -jnp.inf, work)
    es = [jnp.exp(v - vals[0]) for v in vals]
    denom = es[0]
    for e in es[1:]:
        denom = denom + e
    sel = jnp.zeros(logits.shape, jnp.float32)
    for hot in hots:
        sel = sel + jnp.where(hot, 1.0, 0.0)
    rank = _dot(tri_ref[...], sel.astype(MXU_DTYPE))
    route = jnp.zeros(logits.shape, jnp.int32)
    gate = jnp.zeros(logits.shape, jnp.float32)
    for k in range(TOP_K):
        r_k = jnp.sum(jnp.where(hots[k], rank, 0.0), axis=1, keepdims=True).astype(jnp.int32)
        route = route + jnp.where(lane == k, ids[k], 0) + jnp.where(lane == TOP_K + k, r_k, 0)
        gate = gate + jnp.where(lane == k, es[k] / denom, 0.0)
    route_ref[...] = route
    gate_ref[...] = gate
    cnt = jnp.sum(sel, axis=0, keepdims=True).astype(jnp.int32)
    cnt_ref[0] = jnp.broadcast_to(cnt, (SUBLANES, LANES))


def _out_proj(x2, oa, ob, wa, wb, g2, wr, br):
    t, d = x2.shape
    tm = min(PROJ_TM, t)
    row = lambda i: (i, 0)
    fixed = lambda i: (0, 0)
    tri = (jnp.arange(tm)[:, None] > jnp.arange(tm)[None, :]).astype(MXU_DTYPE)
    return pl.pallas_call(
        _outproj_kernel,
        grid=(t // tm,),
        in_specs=[
            pl.BlockSpec((tm, d), row),
            pl.BlockSpec((tm, A_WIDTH), row),
            pl.BlockSpec((tm, B_WIDTH), row),
            pl.BlockSpec((A_WIDTH, d), fixed),
            pl.BlockSpec((B_WIDTH, d), fixed),
            pl.BlockSpec((1, d), fixed),
            pl.BlockSpec((d, LANES), fixed),
            pl.BlockSpec((1, LANES), fixed),
            pl.BlockSpec((tm, tm), fixed),
        ],
        out_specs=[pl.BlockSpec((tm, d), row), pl.BlockSpec((tm, PACK_W), row), pl.BlockSpec((tm, PACK_W), row),
                   pl.BlockSpec((tm, LANES), row), pl.BlockSpec((tm, LANES), row),
                   pl.BlockSpec((1, SUBLANES, LANES), lambda i: (i, 0, 0))],
        out_shape=[jax.ShapeDtypeStruct((t, d), jnp.float32),
                   jax.ShapeDtypeStruct((t, PACK_W), jnp.int32),
                   jax.ShapeDtypeStruct((t, PACK_W), jnp.int32),
                   jax.ShapeDtypeStruct((t, LANES), jnp.int32),
                   jax.ShapeDtypeStruct((t, LANES), jnp.float32),
                   jax.ShapeDtypeStruct((t // tm, SUBLANES, LANES), jnp.int32)],
        compiler_params=pltpu.CompilerParams(dimension_semantics=("parallel",),
                                             vmem_limit_bytes=VMEM_LIMIT),
        name="out_proj",
    )(x2, oa, ob, wa, wb, g2, wr, br, tri)


def _sc_mesh():
    return plsc.VectorSubcoreMesh(core_axis_name="core", subcore_axis_name="subcore")


def _sc_scatter_rows(x, idx, n_rows):
    t, w = x.shape
    nk = idx.shape[0]

    @pl.kernel(out_type=jax.ShapeDtypeStruct((n_rows, w), x.dtype), mesh=_sc_mesh(), scratch_types=[])
    def scatter(x_hbm, i_hbm, o_hbm):
        def body(x_vmem, i_vmem):
            pltpu.sync_copy(x_vmem, o_hbm.at[i_vmem.at[0]])

        pltpu.emit_pipeline(
            body, grid=(t // SC_WINDOW, nk),
            in_specs=[pl.BlockSpec((SC_WINDOW, w), lambda i, k: (i, 0)),
                      pl.BlockSpec((1, SC_WINDOW), lambda i, k: (k, i))],
            out_specs=[],
            core_axis_name=("core", "subcore"),
            dimension_semantics=(pltpu.PARALLEL, pltpu.ARBITRARY),
        )(x_hbm, i_hbm)

    return scatter(x, idx)


def _sc_gather_rows(x, idx):
    n = idx.shape[0]
    w = x.shape[1]

    @pl.kernel(out_type=jax.ShapeDtypeStruct((n, w), x.dtype), mesh=_sc_mesh())
    def gather(x_hbm, i_hbm, o_hbm):
        def body(i_vmem, o_vmem):
            pltpu.sync_copy(x_hbm.at[i_vmem.at[0]], o_vmem)

        pltpu.emit_pipeline(
            body, grid=(n // SC_WINDOW,),
            in_specs=[pl.BlockSpec((1, SC_WINDOW), lambda i: (0, i))],
            out_specs=[pl.BlockSpec((SC_WINDOW, w), lambda i: (i, 0))],
            core_axis_name=("core", "subcore"),
            dimension_semantics=(pltpu.PARALLEL,),
        )(i_hbm, o_hbm)

    return gather(x, idx.reshape(1, n))


def _route_plan(cnt, route, ne, nch):
    c = MOE_C
    tiles = cnt.shape[0]
    tm = route.shape[0] // tiles
    tot = jnp.sum(cnt, axis=0)
    nchunk = (tot + c - 1) // c
    cstart = jnp.cumsum(nchunk) - nchunk
    tile_base = (cstart * c)[None, :] + jnp.cumsum(cnt, axis=0) - cnt
    eid = route[:, :TOP_K]
    rank = route[:, TOP_K:2 * TOP_K]
    base_tok = jnp.repeat(tile_base, tm, axis=0)
    hot = eid[:, :, None] == jnp.arange(LANES, dtype=jnp.int32)[None, None, :]
    pos = jnp.sum(jnp.where(hot, base_tok[:, None, :], 0), axis=-1) + rank
    ci = jnp.arange(nch, dtype=jnp.int32)
    cend = jnp.cumsum(nchunk)[:ne]
    ce = jnp.minimum(jnp.sum(ci[:, None] >= cend[None, :], axis=1), ne - 1).astype(jnp.int32)
    cn = jnp.clip(tot[ce] - (ci - cstart[ce]) * c, 0, c)
    cn = jnp.where(ci < cend[ne - 1], cn, 0).astype(jnp.int32)
    return pos.T, ce, cn


def _expert_kernel(ce_ref, cn_ref, xa_ref, xb_ref, w1_ref, b1_ref, w2_ref, b2_ref, ya_ref, yb_ref):
    n = cn_ref[pl.program_id(0)]

    @pl.when(n == 0)
    def _():
        ya_ref[...] = jnp.zeros_like(ya_ref)
        yb_ref[...] = jnp.zeros_like(yb_ref)

    @pl.when(n > 0)
    def _():
        _expert_chunk(n, xa_ref, xb_ref, w1_ref, b1_ref, w2_ref, b2_ref, ya_ref, yb_ref)


def _expert_chunk(n, xa_ref, xb_ref, w1_ref, b1_ref, w2_ref, b2_ref, ya_ref, yb_ref):
    rows = xa_ref.shape[0]
    row = lax.broadcasted_iota(jnp.int32, (rows, 1), 0)
    h2 = jnp.where(row < n, _unpack(xa_ref[...], xb_ref[...]), 0.0).astype(MXU_DTYPE)
    even = (lax.broadcasted_iota(jnp.int32, (rows, LANES), 1) % 2) == 0
    ff = w2_ref.shape[1]
    z = _dot(h2, w1_ref[0]) + b1_ref[0]
    acts = []
    for m in range(ff // LANES):
        za = z[:, 2 * m * LANES:(2 * m + 1) * LANES]
        zb = z[:, (2 * m + 1) * LANES:(2 * m + 2) * LANES]
        glu = jnp.where(even, za, pltpu.roll(zb, 1, 1))
        lin = jnp.where(even, pltpu.roll(za, LANES - 1, 1), zb)
        glu = jnp.minimum(glu, SWIGLU_LIMIT)
        lin = jnp.clip(lin, -SWIGLU_LIMIT, SWIGLU_LIMIT)
        act = glu * (1.0 / (1.0 + jnp.exp(-SWIGLU_ALPHA * glu))) * (lin + 1.0)
        acts.append(act.astype(MXU_DTYPE))
    y = _dot(jnp.concatenate(acts, axis=1), w2_ref[0]) + b2_ref[0]
    yw = _pack(y.astype(MXU_DTYPE).astype(jnp.float32))
    ya_ref[...] = yw[:, :PACK_W]
    yb_ref[...] = yw[:, PACK_W:]


def _pair_rows(w2):
    ne, ff, d = w2.shape
    half = LANES // 2
    return w2.reshape(ne, ff // LANES, 2, half, d).swapaxes(2, 3).reshape(ne, ff, d)


def _experts(ce, cn, xa, xb, w1, b1, w2, b2):
    nch = ce.shape[0]
    ne, d, f2 = w1.shape
    ff = f2 // 2
    c = MOE_C
    rows = lambda i, ce_, cn_: (i, 0)
    of_expert = lambda i, ce_, cn_: (ce_[i], 0, 0)
    out = jax.ShapeDtypeStruct((nch * c, PACK_W), jnp.int32)
    return pl.pallas_call(
        _expert_kernel,
        grid_spec=pltpu.PrefetchScalarGridSpec(
            num_scalar_prefetch=2,
            grid=(nch,),
            in_specs=[
                pl.BlockSpec((c, PACK_W), rows),
                pl.BlockSpec((c, PACK_W), rows),
                pl.BlockSpec((1, d, f2), of_expert),
                pl.BlockSpec((1, 1, f2), of_expert),
                pl.BlockSpec((1, ff, d), of_expert),
                pl.BlockSpec((1, 1, d), of_expert),
            ],
            out_specs=[pl.BlockSpec((c, PACK_W), rows), pl.BlockSpec((c, PACK_W), rows)],
        ),
        out_shape=[out, out],
        compiler_params=pltpu.CompilerParams(dimension_semantics=("arbitrary",),
                                             vmem_limit_bytes=VMEM_LIMIT),
        name="experts",
    )(ce, cn, xa, xb, w1, b1, w2, b2)


def _combine_kernel(x1_ref, gate_ref, ya_ref, yb_ref, gf_ref, o_ref):
    acc = x1_ref[...]
    gate = gate_ref[...]
    for k in range(TOP_K):
        acc = acc + gate[:, k:k + 1] * _unpack(ya_ref[k], yb_ref[k])
    o_ref[...] = _rms(acc, gf_ref[...])


def _combine(x1, gate, ya, yb, gf):
    t, d = x1.shape
    tm = min(MOE_TM, t)
    row = lambda i: (i, 0)
    picks = lambda i: (0, i, 0)
    return pl.pallas_call(
        _combine_kernel,
        grid=(t // tm,),
        in_specs=[
            pl.BlockSpec((tm, d), row),
            pl.BlockSpec((tm, LANES), row),
            pl.BlockSpec((TOP_K, tm, PACK_W), picks),
            pl.BlockSpec((TOP_K, tm, PACK_W), picks),
            pl.BlockSpec((1, d), lambda i: (0, 0)),
        ],
        out_specs=pl.BlockSpec((tm, d), row),
        out_shape=jax.ShapeDtypeStruct((t, d), jnp.float32),
        compiler_params=pltpu.CompilerParams(dimension_semantics=("parallel",),
                                             vmem_limit_bytes=VMEM_LIMIT),
        name="combine",
    )(x1, gate, ya, yb, gf)


def kernel(x, norm1_gain, w_in, rel_bias, idx_k_ln_gain, idx_k_ln_bias, ret_gn_gain,
           w_out, norm2_gain, w_router, b_router, w_mlp1, b_mlp1, w_mlp2, b_mlp2, final_gain):
    bsz, s, d = x.shape
    depth = w_in.shape[0]
    topk = min(IDX_TOPK_MAX, s // 4)
    assert depth == 1, "the final RMSNorm is fused into the expert kernel of the single layer"
    assert s % DSA_T == 0 and s % RET_C == 0 and topk <= DSA_T and DSA_T % CHUNK == 0
    assert s % min(PROJ_TM, s) == 0 and (bsz * s) % MOE_TM == 0
    ne = w_router.shape[-1]
    assert ne <= LANES and d == 4 * PACK_W
    assert (bsz * s) % SC_WINDOW == 0 and (bsz * s * TOP_K) % MOE_C == 0
    f32 = jnp.float32
    x2 = x.reshape(bsz * s, d).astype(f32)
    cos, sin = _rotary_tables(s)
    bias = _bias_tiles(rel_bias, DSA_T)

    for layer in range(depth):
        wp = _pack_proj_weight(w_in[layer])
        qa, ka, va, qi, ki, wi, qb, kb, vb, sg = _in_proj(
            x2, norm1_gain[layer][None, :].astype(f32), wp, cos, sin,
            idx_k_ln_gain[layer][None, :].astype(f32), idx_k_ln_bias[layer][None, :].astype(f32), s)

        r3 = lambda a: a.reshape(bsz, s, a.shape[-1])
        out_b = _retention(r3(qb), r3(kb), r3(vb), r3(sg),
                           ret_gn_gain[layer][None, :].astype(f32), min(RET_C, s))
        tr = lambda a: jnp.swapaxes(r3(a), 1, 2)
        ones_rows = jnp.zeros((bsz, A_HEADS, V_ROWS - A_HEAD_DIM, s), MXU_DTYPE).at[:, :, 0, :].set(1.0)
        vt = jnp.concatenate([tr(va).reshape(bsz, A_HEADS, A_HEAD_DIM, s), ones_rows],
                             axis=2).reshape(bsz, A_HEADS * V_ROWS, s)
        out_a = _dsa(tr(qa), tr(qi), tr(wi), r3(ka), vt, r3(ki), bias, topk)

        wo = w_out[layer].astype(MXU_DTYPE)
        wr = jnp.pad(w_router[layer], ((0, 0), (0, LANES - ne))).astype(MXU_DTYPE)
        br = jnp.pad(b_router[layer].astype(f32), (0, LANES - ne), constant_values=-jnp.inf)[None, :]
        x1, ha, hb, route, gate, cnt = _out_proj(
            x2, out_a.reshape(bsz * s, A_WIDTH), out_b.reshape(bsz * s, B_WIDTH),
            wo[:A_WIDTH], wo[A_WIDTH:], norm2_gain[layer][None, :].astype(f32), wr, br)

        nch = (bsz * s * TOP_K) // MOE_C + ne
        pos, ce, cn = _route_plan(cnt[:, 0, :], route, ne, nch)
        xa = _sc_scatter_rows(ha, pos, nch * MOE_C)
        xb = _sc_scatter_rows(hb, pos, nch * MOE_C)
        ya, yb = _experts(ce, cn, xa, xb, w_mlp1[layer].astype(MXU_DTYPE), b_mlp1[layer].astype(f32)[:, None, :],
                          _pair_rows(w_mlp2[layer]).astype(MXU_DTYPE), b_mlp2[layer].astype(f32)[:, None, :])
        picks = pos.reshape(-1)
        ga = _sc_gather_rows(ya, picks).reshape(TOP_K, bsz * s, PACK_W)
        gb = _sc_gather_rows(yb, picks).reshape(TOP_K, bsz * s, PACK_W)
        x2 = _combine(x1, gate, ga, gb, final_gain[None, :].astype(f32))

    return x2.reshape(bsz, s, d).astype(x.dtype)
```

```python
import functools
import math

import numpy as np
import jax
import jax.numpy as jnp
from jax import lax
from jax.experimental import pallas as pl
from jax.experimental.pallas import tpu as pltpu
from jax.experimental.pallas import tpu_sc as plsc

CHUNK = 64
A_HEADS = 8
A_HEAD_DIM = 64
A_WIDTH = A_HEADS * A_HEAD_DIM
IDX_HEADS = 8
IDX_DIM = 64
IDX_TOPK_MAX = 256
B_HEADS = 4
B_QK_DIM = 64
B_V_DIM = 128
B_QK_WIDTH = B_HEADS * B_QK_DIM
B_WIDTH = B_HEADS * B_V_DIM
REL_BUCKETS = 32
REL_MAX_DIST = 128
TOP_K = 4
SWIGLU_LIMIT = 7.0
SWIGLU_ALPHA = 1.702
ROPE_BASE = 10000.0
RMS_EPS = 1e-5
LN_EPS = 1e-6

LANES = 128
SUBLANES = 8
VMEM_LIMIT = 56 * 1024 * 1024

MXU_DTYPE = jnp.bfloat16

INT_MIN = -(2 ** 31)
INT_MAX = 2 ** 31 - 1
F32_MIN = float(np.finfo(np.float32).min)
LOG2E = math.log2(math.e)

PROJ_TM = 512
RET_C = 256
DSA_T = 256
COUNT_LANES = 4
V_ROWS = A_HEAD_DIM + 16
MOE_TM = 512
MOE_C = 512
SC_WINDOW = 128
PACK_W = 256


def _dot(a, b):
    return jnp.dot(a, b, preferred_element_type=jnp.float32)


def _dot_nt(a, b):
    return lax.dot_general(a, b, (((1,), (1,)), ((), ())), preferred_element_type=jnp.float32)


def _rms(x, gain):
    return x * lax.rsqrt(jnp.mean(x * x, axis=-1, keepdims=True) + RMS_EPS) * gain


_SEG = {}
_off = 0
for _name, _w in (("qa", A_WIDTH), ("ka", A_WIDTH), ("va", A_WIDTH), ("qi", IDX_HEADS * IDX_DIM),
                  ("ki", LANES), ("wi", LANES),
                  ("qb", B_HEADS * LANES), ("qbr", B_HEADS * LANES),
                  ("kb", B_HEADS * LANES), ("kbr", B_HEADS * LANES),
                  ("vb", B_WIDTH), ("gb", B_WIDTH)):
    _SEG[_name] = (_off, _off + _w)
    _off += _w
PROJ_N = _off


def _proj_kernel(x_ref, g_ref, w_ref, cos_ref, sin_ref, lng_ref, lnb_ref,
                 qa_ref, ka_ref, va_ref, qi_ref, ki_ref, wi_ref, qb_ref, kb_ref, vb_ref, sg_ref):
    h = _rms(x_ref[...], g_ref[...]).astype(MXU_DTYPE)

    def seg(name):
        a, b = _SEG[name]
        return _dot(h, w_ref[:, a:b])

    qa_ref[...] = (seg("qa") * (A_HEAD_DIM ** -0.5 * LOG2E)).astype(qa_ref.dtype)
    ka_ref[...] = seg("ka").astype(ka_ref.dtype)
    va_ref[...] = seg("va").astype(va_ref.dtype)
    qi_ref[...] = seg("qi").astype(qi_ref.dtype)
    ki = seg("ki")[:, :IDX_DIM]
    mu = jnp.mean(ki, axis=-1, keepdims=True)
    var = jnp.mean(jnp.square(ki - mu), axis=-1, keepdims=True)
    ki = (ki - mu) * lax.rsqrt(var + LN_EPS) * lng_ref[...] + lnb_ref[...]
    ki_ref[...] = ki.astype(ki_ref.dtype)
    wi_ref[...] = seg("wi")[:, :IDX_HEADS] * ((IDX_HEADS ** -0.5) * (IDX_DIM ** -0.5))
    cos = cos_ref[...]
    sin = sin_ref[...]
    qb_ref[...] = (seg("qb") * cos + seg("qbr") * sin).astype(qb_ref.dtype)
    kb_ref[...] = ((seg("kb") * cos + seg("kbr") * sin) * (B_QK_DIM ** -0.5)).astype(kb_ref.dtype)
    vb_ref[...] = seg("vb").astype(vb_ref.dtype)
    g = seg("gb")
    sg_ref[...] = g / (1.0 + jnp.exp(-g))


def _pack_proj_weight(w_in):
    d = w_in.shape[0]
    sizes = (A_WIDTH, A_WIDTH, A_WIDTH, IDX_HEADS * IDX_DIM, IDX_DIM, IDX_HEADS,
             B_QK_WIDTH, B_QK_WIDTH, B_WIDTH, B_WIDTH)
    cuts = np.cumsum(sizes)[:-1].tolist()
    wqa, wka, wva, wqi, wki, wwi, wqb, wkb, wvb, wgb = jnp.split(w_in, cuts, axis=-1)

    def pad_cols(w, n):
        return jnp.pad(w, ((0, 0), (0, n - w.shape[1])))

    def head_slots(w):
        w = w.reshape(d, B_HEADS, B_QK_DIM)
        return jnp.pad(w, ((0, 0), (0, 0), (0, LANES - B_QK_DIM))).reshape(d, B_HEADS * LANES)

    def rot_half(w):
        w = w.reshape(d, B_HEADS, 2, B_QK_DIM // 2)
        return jnp.stack([-w[:, :, 1], w[:, :, 0]], axis=2).reshape(d, B_QK_WIDTH)

    parts = [wqa, wka, wva, wqi, pad_cols(wki, LANES), pad_cols(wwi, LANES),
             head_slots(wqb), head_slots(rot_half(wqb)), head_slots(wkb), head_slots(rot_half(wkb)),
             wvb, wgb]
    return jnp.concatenate(parts, axis=-1).astype(MXU_DTYPE)


def _rotary_tables(s):
    half = B_QK_DIM // 2
    inv = 1.0 / (ROPE_BASE ** jnp.linspace(0.0, 1.0, half, dtype=jnp.float32))
    ang = jnp.arange(s, dtype=jnp.int32).astype(jnp.float32)[:, None] * inv[None, :]

    def slots(t):
        t = jnp.concatenate([t, t, jnp.zeros((s, LANES - B_QK_DIM), jnp.float32)], axis=-1)
        return jnp.tile(t, (1, B_HEADS))

    return slots(jnp.cos(ang)), slots(jnp.sin(ang))


def _in_proj(x2, g1, wp, cos, sin, lng, lnb, s):
    t, d = x2.shape
    tm = min(PROJ_TM, s)
    nt = t // tm
    ns = s // tm
    row = lambda i: (i, 0)
    fixed = lambda i: (0, 0)
    pos = lambda i: (i % ns, 0)
    bw = B_HEADS * LANES
    out_shapes = [
        jax.ShapeDtypeStruct((t, A_WIDTH), MXU_DTYPE),
        jax.ShapeDtypeStruct((t, A_WIDTH), MXU_DTYPE),
        jax.ShapeDtypeStruct((t, A_WIDTH), MXU_DTYPE),
        jax.ShapeDtypeStruct((t, IDX_HEADS * IDX_DIM), MXU_DTYPE),
        jax.ShapeDtypeStruct((t, IDX_DIM), MXU_DTYPE),
        jax.ShapeDtypeStruct((t, IDX_HEADS), jnp.float32),
        jax.ShapeDtypeStruct((t, bw), MXU_DTYPE),
        jax.ShapeDtypeStruct((t, bw), MXU_DTYPE),
        jax.ShapeDtypeStruct((t, B_WIDTH), MXU_DTYPE),
        jax.ShapeDtypeStruct((t, B_WIDTH), jnp.float32),
    ]
    out_specs = [pl.BlockSpec((tm, o.shape[1]), row) for o in out_shapes]
    return pl.pallas_call(
        _proj_kernel,
        grid=(nt,),
        in_specs=[
            pl.BlockSpec((tm, d), row),
            pl.BlockSpec((1, d), fixed),
            pl.BlockSpec((d, PROJ_N), fixed),
            pl.BlockSpec((tm, bw), pos),
            pl.BlockSpec((tm, bw), pos),
            pl.BlockSpec((1, IDX_DIM), fixed),
            pl.BlockSpec((1, IDX_DIM), fixed),
        ],
        out_specs=out_specs,
        out_shape=out_shapes,
        compiler_params=pltpu.CompilerParams(dimension_semantics=("parallel",),
                                             vmem_limit_bytes=VMEM_LIMIT),
        name="in_proj",
    )(x2, g1, wp, cos, sin, lng, lnb)


def _ret_kernel(q_ref, k_ref, v_ref, sg_ref, gain_ref, decay_ref, zeta_ref, xi_ref, cd_ref,
                o_ref, state_ref):
    @pl.when(pl.program_id(1) == 0)
    def _():
        state_ref[...] = jnp.zeros_like(state_ref)

    for h in range(B_HEADS):
        sl = slice(h * LANES, (h + 1) * LANES)
        q = q_ref[0, :, sl]
        k = k_ref[0, :, sl]
        v = v_ref[0, :, sl]
        state = state_ref[h]
        scores = _dot_nt(q, k) * decay_ref[h]
        intra = _dot(scores.astype(MXU_DTYPE), v)
        cross = _dot(q, state.astype(MXU_DTYPE)) * xi_ref[h]
        y = intra + cross
        kz = (k.astype(jnp.float32) * zeta_ref[h]).T.astype(MXU_DTYPE)
        state_ref[h] = cd_ref[h] * state + _dot(kz, v)
        mu = jnp.mean(y, axis=-1, keepdims=True)
        var = jnp.mean(jnp.square(y - mu), axis=-1, keepdims=True)
        yn = (y - mu) * lax.rsqrt(var + LN_EPS) * gain_ref[:, sl]
        o_ref[0, :, sl] = (sg_ref[0, :, sl] * yn).astype(o_ref.dtype)


def _retention(qb, kb, vb, sg, gain, c):
    b, s, bw = qb.shape
    f32 = jnp.float32
    log_g = jnp.log(1.0 - 2.0 ** (-5.0 - jnp.arange(B_HEADS, dtype=f32)))
    n = jnp.arange(c, dtype=f32)
    diff = n[:, None] - n[None, :]
    decay = jnp.where(diff >= 0, jnp.exp(log_g[:, None, None] * jnp.maximum(diff, 0.0)), 0.0)
    zeta = jnp.exp(log_g[:, None] * (c - 1.0 - n)[None, :])[:, :, None]
    xi = jnp.exp(log_g[:, None] * (n + 1.0)[None, :])[:, :, None]
    cd = jnp.broadcast_to(jnp.exp(log_g * c)[:, None, None], (B_HEADS, 1, LANES))
    blk = lambda bi, ci: (bi, ci, 0)
    fix3 = lambda bi, ci: (0, 0, 0)
    return pl.pallas_call(
        _ret_kernel,
        grid=(b, s // c),
        in_specs=[
            pl.BlockSpec((1, c, bw), blk),
            pl.BlockSpec((1, c, bw), blk),
            pl.BlockSpec((1, c, B_WIDTH), blk),
            pl.BlockSpec((1, c, B_WIDTH), blk),
            pl.BlockSpec((1, B_WIDTH), lambda bi, ci: (0, 0)),
            pl.BlockSpec((B_HEADS, c, c), fix3),
            pl.BlockSpec((B_HEADS, c, 1), fix3),
            pl.BlockSpec((B_HEADS, c, 1), fix3),
            pl.BlockSpec((B_HEADS, 1, LANES), fix3),
        ],
        out_specs=pl.BlockSpec((1, c, B_WIDTH), blk),
        out_shape=jax.ShapeDtypeStruct((b, s, B_WIDTH), MXU_DTYPE),
        scratch_shapes=[pltpu.VMEM((B_HEADS, LANES, LANES), jnp.float32)],
        compiler_params=pltpu.CompilerParams(dimension_semantics=("parallel", "arbitrary"),
                                             vmem_limit_bytes=VMEM_LIMIT),
        name="retention",
    )(qb, kb, vb, sg, gain, decay, zeta, xi, cd)


def _t5_bucket(rel):
    nb = REL_BUCKETS // 2
    max_exact = nb // 2
    ret = jnp.where(rel > 0, nb, 0)
    n = jnp.abs(rel)
    nf = jnp.maximum(n, 1).astype(jnp.float32)
    large = max_exact + (jnp.log(nf / max_exact) / math.log(REL_MAX_DIST / max_exact)
                         * (nb - max_exact)).astype(jnp.int32)
    large = jnp.minimum(large, nb - 1)
    return ret + jnp.where(n < max_exact, n, large)


def _bias_tiles(rel_bias, t):
    i = jnp.arange(t, dtype=jnp.int32)
    rel_d = i[:, None] - i[None, :]
    rb = rel_bias.astype(jnp.float32)
    far = rb[_t5_bucket(jnp.int32(-(t + 1)))]

    def tile(rel):
        hot = jax.nn.one_hot(_t5_bucket(rel), REL_BUCKETS, dtype=jnp.float32)
        return (jnp.einsum("kqb,bh->hkq", hot, rb, precision=lax.Precision.HIGHEST) - far[:, None, None]) * LOG2E

    near = tile(rel_d - t)
    return jnp.stack([jnp.zeros_like(near), near, tile(rel_d)])


def _dsa_kernel(qat_ref, qit_ref, wit_ref, k_ref, vt_ref, ki_ref, bias_ref, o_ref,
                keys_ref, gmax_ref, mb_ref, sbuf_ref, m_ref, al_ref, acc_ref, *, topk):
    t = DSA_T
    i32 = jnp.int32
    qblk = pl.program_id(1)
    ntile = qblk + 1
    qcol = lax.broadcasted_iota(i32, (1, t), 1)
    lim_local = (qcol // CHUNK + 1) * CHUNK
    keep_all = (qblk * t + lim_local) <= topk
    krow = lax.broadcasted_iota(i32, (t, t), 0)

    def tile_off(j):
        return pl.multiple_of(j * t, t)

    qit = [qit_ref[0, h * IDX_DIM:(h + 1) * IDX_DIM, :] for h in range(IDX_HEADS)]
    wit = [wit_ref[0, h:h + 1, :] for h in range(IDX_HEADS)]

    def score_tile(j, diag):
        off = tile_off(j)
        half = t // 2
        for r0 in (0, half):
            start = pl.multiple_of(off + r0, half)
            kt = ki_ref[0, pl.ds(start, half), :]
            sc = jnp.zeros((half, t), jnp.float32)
            for h in range(IDX_HEADS):
                sc = sc + wit[h] * jnp.maximum(_dot(kt, qit[h]), 0.0)
            sc = jnp.where(sc == 0.0, 0.0, sc)
            bits = pltpu.bitcast(sc, i32)
            key = bits ^ ((bits >> 31) & INT_MAX)
            if diag:
                key = jnp.where(lax.broadcasted_iota(i32, (half, t), 0) + r0 < lim_local, key, INT_MIN)
            keys_ref[pl.ds(start, half), :] = key
            gmax_ref[r0:r0 + half, :] = jnp.maximum(gmax_ref[r0:r0 + half, :], key)

    def far_scores(j, c):
        score_tile(j, False)
        return c

    gmax_ref[...] = jnp.full(gmax_ref.shape, INT_MIN, i32)
    lax.fori_loop(0, qblk, far_scores, 0)
    score_tile(qblk, True)

    sub = lax.broadcasted_iota(i32, (SUBLANES, t), 0)

    def rows8(v):
        return jnp.broadcast_to(v, (SUBLANES, t))

    def count(pred):
        def body(j, cnts):
            off = tile_off(j)
            tile = keys_ref[pl.ds(off, t), :]
            cnts = list(cnts)
            for i, r in enumerate(range(0, t, SUBLANES)):
                a = i % len(cnts)
                cnts[a] = cnts[a] + jnp.where(pred(tile[r:r + SUBLANES], sub + (off + r)), 1, 0)
            return tuple(cnts)
        zero = jnp.zeros((SUBLANES, t), i32)
        cnts = lax.fori_loop(0, ntile, body, (zero,) * COUNT_LANES)
        return jnp.sum(functools.reduce(lambda a, b: a + b, cnts), axis=0, keepdims=True)

    def unsettled(lo, hi):
        return jnp.max(jnp.where(hi != lo + 1, 1.0, 0.0))

    def bis_cond(c):
        it, active, _, _ = c
        return jnp.logical_and(it < 32, active > 0.0)

    def bis_pass(lo, hi):
        mid = (lo >> 1) + (hi >> 1) + (lo & hi & 1)
        mid8 = rows8(mid)
        n = count(lambda k, idx: k >= mid8)
        ge = n >= topk
        lo = jnp.where(ge, mid, lo)
        hi = jnp.where(n == topk, mid + 1, jnp.where(ge, hi, mid))
        return lo, hi

    def bis_body(c):
        it, _, lo, hi = c
        lo, hi = bis_pass(*bis_pass(lo, hi))
        return it + 2, unsettled(lo, hi), lo, hi

    gmax = gmax_ref[...]
    kmax = jnp.max(gmax, axis=0, keepdims=True)
    lo0 = jnp.where(keep_all, INT_MIN, jnp.min(gmax, axis=0, keepdims=True))
    hi0 = jnp.where(keep_all, INT_MIN + 1, jnp.where(kmax == INT_MAX, INT_MAX, kmax + 1))
    _, _, lo, _ = lax.while_loop(bis_cond, bis_body, (jnp.int32(0), unsettled(lo0, hi0), lo0, hi0))
    tau = jnp.where(keep_all, INT_MIN + 1, lo)
    tau8 = rows8(tau)
    n_ge = count(lambda k, idx: k >= tau8)
    tie = jnp.logical_and(n_ge > topk, jnp.logical_not(keep_all))

    @pl.when(jnp.max(jnp.where(tie, 1, 0)) > 0)
    def _():
        n_eq_keep = topk - count(lambda k, idx: k > tau8)

        def bis_pos(_, carry):
            plo, phi = carry
            pmid = (plo + phi) >> 1
            pmid8 = rows8(pmid)
            ok = count(lambda k, idx: jnp.logical_and(k == tau8, idx < pmid8)) <= n_eq_keep
            return jnp.where(ok, pmid, plo), jnp.where(ok, phi, pmid)

        smax = pl.num_programs(1) * t
        nbits = int(math.ceil(math.log2(keys_ref.shape[0]))) + 1
        plo, _ = lax.fori_loop(0, nbits, bis_pos,
                               (jnp.zeros((1, t), i32), jnp.zeros((1, t), i32) + smax))
        cut = jnp.where(tie, plo, smax)

        def drop(j, c):
            off = tile_off(j)
            k = keys_ref[pl.ds(off, t), :]
            dead = jnp.logical_and(k == tau, (krow + j * t) >= cut)
            keys_ref[pl.ds(off, t), :] = jnp.where(dead, INT_MIN, k)
            return c

        lax.fori_loop(0, ntile, drop, 0)

    m_ref[...] = jnp.full(m_ref.shape, F32_MIN, jnp.float32)
    al_ref[...] = jnp.ones(al_ref.shape, jnp.float32)
    acc_ref[...] = jnp.zeros(acc_ref.shape, jnp.float32)
    srow = lax.broadcasted_iota(i32, (LANES, t), 0)
    qmt = []
    for h in range(A_HEADS):
        pair = qat_ref[0, (h // 2) * LANES:(h // 2 + 1) * LANES, :]
        qmt.append(jnp.where((srow // A_HEAD_DIM) == (h % 2), pair, jnp.zeros_like(pair)))

    def bias_kind(j):
        return jnp.where(j >= qblk, 2, jnp.where(j >= qblk - 1, 1, 0))

    def set_mask(j):
        mb_ref[...] = jnp.where(keys_ref[pl.ds(tile_off(j), t), :] >= tau, 0.0, -jnp.inf)

    def consume(j, h):
        m_h = m_ref[h]
        a_h = al_ref[h]
        p = jnp.exp2(sbuf_ref[h] - m_h)
        pv = _dot(vt_ref[0, h * V_ROWS:(h + 1) * V_ROWS, pl.ds(tile_off(j), t)], p.astype(MXU_DTYPE))
        acc_ref[h] = a_h * acc_ref[h] + pv

    def produce(j, h, kind):
        g = h // 2
        s = _dot(k_ref[0, pl.ds(tile_off(j), t), g * LANES:(g + 1) * LANES], qmt[h]) + mb_ref[...]
        if kind is not None:
            s = s + bias_ref[kind, h]
        m_h = m_ref[h]
        m_new = jnp.maximum(m_h, jnp.max(s, axis=0, keepdims=True))
        al_ref[h] = jnp.exp2(m_h - m_new)
        m_ref[h] = m_new
        sbuf_ref[h] = s

    set_mask(0)
    for h in range(A_HEADS):
        produce(0, h, bias_kind(0))

    def step(j, kind):
        set_mask(j + 1)
        for h in range(A_HEADS):
            consume(j, h)
            produce(j + 1, h, kind)

    def far_step(j, c):
        step(j, None)
        return c

    def near_step(j, c):
        step(j, bias_kind(j + 1))
        return c

    n_far = jnp.maximum(qblk - 2, 0)
    lax.fori_loop(0, n_far, far_step, 0)
    lax.fori_loop(n_far, qblk, near_step, 0)
    for h in range(A_HEADS):
        consume(qblk, h)

    out_t = jnp.concatenate([acc_ref[h, :A_HEAD_DIM] / acc_ref[h, A_HEAD_DIM:A_HEAD_DIM + 1]
                             for h in range(A_HEADS)], axis=0)
    o_ref[0] = out_t.T.astype(o_ref.dtype)


def _dsa(qat, qit, wit, k, vt, ki, bias, topk):
    b, _, s = qat.shape
    t = DSA_T
    qcols = lambda bi, qi_: (bi, 0, qi_)
    per_b = lambda bi, qi_: (bi, 0, 0)
    once = pl.Buffered(1)
    return pl.pallas_call(
        functools.partial(_dsa_kernel, topk=topk),
        grid=(b, s // t),
        in_specs=[
            pl.BlockSpec((1, A_WIDTH, t), qcols),
            pl.BlockSpec((1, IDX_HEADS * IDX_DIM, t), qcols),
            pl.BlockSpec((1, IDX_HEADS, t), qcols),
            pl.BlockSpec((1, s, A_WIDTH), per_b, pipeline_mode=once),
            pl.BlockSpec((1, A_HEADS * V_ROWS, s), per_b, pipeline_mode=once),
            pl.BlockSpec((1, s, IDX_DIM), per_b, pipeline_mode=once),
            pl.BlockSpec((3, A_HEADS, t, t), lambda bi, qi_: (0, 0, 0, 0), pipeline_mode=once),
        ],
        out_specs=pl.BlockSpec((1, t, A_WIDTH), lambda bi, qi_: (bi, qi_, 0)),
        out_shape=jax.ShapeDtypeStruct((b, s, A_WIDTH), MXU_DTYPE),
        scratch_shapes=[
            pltpu.VMEM((s, t), jnp.int32),
            pltpu.VMEM((t, t), jnp.int32),
            pltpu.VMEM((t, t), jnp.float32),
            pltpu.VMEM((A_HEADS, t, t), jnp.float32),
            pltpu.VMEM((A_HEADS, 1, t), jnp.float32),
            pltpu.VMEM((A_HEADS, 1, t), jnp.float32),
            pltpu.VMEM((A_HEADS, V_ROWS, t), jnp.float32),
        ],
        compiler_params=pltpu.CompilerParams(dimension_semantics=("parallel", "arbitrary"),
                                             vmem_limit_bytes=VMEM_LIMIT),
        name="dsa",
    )(qat, qit, wit, k, vt, ki, bias)


HI16 = -65536


def _pack(v):
    bits = pltpu.bitcast(v, jnp.int32)
    n = v.shape[1] // 2
    return lax.shift_right_logical(bits[:, :n], 16) | (bits[:, n:] & HI16)


def _unpack(wa, wb):
    w = jnp.concatenate([wa, wb], axis=1)
    lo = pltpu.bitcast(w << 16, jnp.float32)
    hi = pltpu.bitcast(w & HI16, jnp.float32)
    return jnp.concatenate([lo, hi], axis=1)


def _outproj_kernel(x_ref, oa_ref, ob_ref, wa_ref, wb_ref, g2_ref, wr_ref, br_ref, tri_ref,
                    x1_ref, ha_ref, hb_ref, route_ref, gate_ref, cnt_ref):
    x1 = x_ref[...] + _dot(oa_ref[...], wa_ref[...]) + _dot(ob_ref[...], wb_ref[...])
    x1_ref[...] = x1
    h2 = _rms(x1, g2_ref[...])
    h2m = h2.astype(MXU_DTYPE)
    hw = _pack(h2m.astype(jnp.float32))
    ha_ref[...] = hw[:, :PACK_W]
    hb_ref[...] = hw[:, PACK_W:]
    logits = _dot(h2m, wr_ref[...]) + br_ref[...]
    lane = lax.broadcasted_iota(jnp.int32, logits.shape, 1)
    work = logits
    vals, hots, ids = [], [], []
    for _ in range(TOP_K):
        mx = jnp.max(work, axis=1, keepdims=True)
        idx = jnp.min(jnp.where(work == mx, lane, LANES), axis=1, keepdims=True)
        hot = lane == idx
        vals.append(mx)
        hots.append(hot)
        ids.append(idx)
        work = jnp.where(hot, -jnp.inf, work)
    es = [jnp.exp(v - vals[0]) for v in vals]
    denom = es[0]
    for e in es[1:]:
        denom = denom + e
    sel = jnp.zeros(logits.shape, jnp.float32)
    for hot in hots:
        sel = sel + jnp.where(hot, 1.0, 0.0)
    rank = _dot(tri_ref[...], sel.astype(MXU_DTYPE))
    route = jnp.zeros(logits.shape, jnp.int32)
    gate = jnp.zeros(logits.shape, jnp.float32)
    for k in range(TOP_K):
        r_k = jnp.sum(jnp.where(hots[k], rank, 0.0), axis=1, keepdims=True).astype(jnp.int32)
        route = route + jnp.where(lane == k, ids[k], 0) + jnp.where(lane == TOP_K + k, r_k, 0)
        gate = gate + jnp.where(lane == k, es[k] / denom, 0.0)
    route_ref[...] = route
    gate_ref[...] = gate
    cnt = jnp.sum(sel, axis=0, keepdims=True).astype(jnp.int32)
    cnt_ref[0] = jnp.broadcast_to(cnt, (SUBLANES, LANES))


def _out_proj(x2, oa, ob, wa, wb, g2, wr, br):
    t, d = x2.shape
    tm = min(PROJ_TM, t)
    row = lambda i: (i, 0)
    fixed = lambda i: (0, 0)
    tri = (jnp.arange(tm)[:, None] > jnp.arange(tm)[None, :]).astype(MXU_DTYPE)
    return pl.pallas_call(
        _outproj_kernel,
        grid=(t // tm,),
        in_specs=[
            pl.BlockSpec((tm, d), row),
            pl.BlockSpec((tm, A_WIDTH), row),
            pl.BlockSpec((tm, B_WIDTH), row),
            pl.BlockSpec((A_WIDTH, d), fixed),
            pl.BlockSpec((B_WIDTH, d), fixed),
            pl.BlockSpec((1, d), fixed),
            pl.BlockSpec((d, LANES), fixed),
            pl.BlockSpec((1, LANES), fixed),
            pl.BlockSpec((tm, tm), fixed),
        ],
        out_specs=[pl.BlockSpec((tm, d), row), pl.BlockSpec((tm, PACK_W), row), pl.BlockSpec((tm, PACK_W), row),
                   pl.BlockSpec((tm, LANES), row), pl.BlockSpec((tm, LANES), row),
                   pl.BlockSpec((1, SUBLANES, LANES), lambda i: (i, 0, 0))],
        out_shape=[jax.ShapeDtypeStruct((t, d), jnp.float32),
                   jax.ShapeDtypeStruct((t, PACK_W), jnp.int32),
                   jax.ShapeDtypeStruct((t, PACK_W), jnp.int32),
                   jax.ShapeDtypeStruct((t, LANES), jnp.int32),
                   jax.ShapeDtypeStruct((t, LANES), jnp.float32),
                   jax.ShapeDtypeStruct((t // tm, SUBLANES, LANES), jnp.int32)],
        compiler_params=pltpu.CompilerParams(dimension_semantics=("parallel",),
                                             vmem_limit_bytes=VMEM_LIMIT),
        name="out_proj",
    )(x2, oa, ob, wa, wb, g2, wr, br, tri)


def _sc_mesh():
    return plsc.VectorSubcoreMesh(core_axis_name="core", subcore_axis_name="subcore")


def _sc_scatter_rows(x, idx, n_rows):
    t, w = x.shape
    nk = idx.shape[0]

    @pl.kernel(out_type=jax.ShapeDtypeStruct((n_rows, w), x.dtype), mesh=_sc_mesh(), scratch_types=[])
    def scatter(x_hbm, i_hbm, o_hbm):
        def body(x_vmem, i_vmem):
            pltpu.sync_copy(x_vmem, o_hbm.at[i_vmem.at[0]])

        pltpu.emit_pipeline(
            body, grid=(t // SC_WINDOW, nk),
            in_specs=[pl.BlockSpec((SC_WINDOW, w), lambda i, k: (i, 0)),
                      pl.BlockSpec((1, SC_WINDOW), lambda i, k: (k, i))],
            out_specs=[],
            core_axis_name=("core", "subcore"),
            dimension_semantics=(pltpu.PARALLEL, pltpu.ARBITRARY),
        )(x_hbm, i_hbm)

    return scatter(x, idx)


def _sc_gather_rows(x, idx):
    n = idx.shape[0]
    w = x.shape[1]

    @pl.kernel(out_type=jax.ShapeDtypeStruct((n, w), x.dtype), mesh=_sc_mesh())
    def gather(x_hbm, i_hbm, o_hbm):
        def body(i_vmem, o_vmem):
            pltpu.sync_copy(x_hbm.at[i_vmem.at[0]], o_vmem)

        pltpu.emit_pipeline(
            body, grid=(n // SC_WINDOW,),
            in_specs=[pl.BlockSpec((1, SC_WINDOW), lambda i: (0, i))],
            out_specs=[pl.BlockSpec((SC_WINDOW, w), lambda i: (i, 0))],
            core_axis_name=("core", "subcore"),
            dimension_semantics=(pltpu.PARALLEL,),
        )(i_hbm, o_hbm)

    return gather(x, idx.reshape(1, n))


def _route_plan(cnt, route, ne, nch):
    c = MOE_C
    tiles = cnt.shape[0]
    tm = route.shape[0] // tiles
    tot = jnp.sum(cnt, axis=0)
    nchunk = (tot + c - 1) // c
    cstart = jnp.cumsum(nchunk) - nchunk
    tile_base = (cstart * c)[None, :] + jnp.cumsum(cnt, axis=0) - cnt
    eid = route[:, :TOP_K]
    rank = route[:, TOP_K:2 * TOP_K]
    base_tok = jnp.repeat(tile_base, tm, axis=0)
    hot = eid[:, :, None] == jnp.arange(LANES, dtype=jnp.int32)[None, None, :]
    pos = jnp.sum(jnp.where(hot, base_tok[:, None, :], 0), axis=-1) + rank
    ci = jnp.arange(nch, dtype=jnp.int32)
    cend = jnp.cumsum(nchunk)[:ne]
    ce = jnp.minimum(jnp.sum(ci[:, None] >= cend[None, :], axis=1), ne - 1).astype(jnp.int32)
    cn = jnp.clip(tot[ce] - (ci - cstart[ce]) * c, 0, c)
    cn = jnp.where(ci < cend[ne - 1], cn, 0).astype(jnp.int32)
    return pos.T, ce, cn


def _expert_kernel(ce_ref, cn_ref, xa_ref, xb_ref, w1_ref, b1_ref, w2_ref, b2_ref, ya_ref, yb_ref):
    n = cn_ref[pl.program_id(0)]

    @pl.when(n == 0)
    def _():
        ya_ref[...] = jnp.zeros_like(ya_ref)
        yb_ref[...] = jnp.zeros_like(yb_ref)

    @pl.when(n > 0)
    def _():
        _expert_chunk(n, xa_ref, xb_ref, w1_ref, b1_ref, w2_ref, b2_ref, ya_ref, yb_ref)


def _expert_chunk(n, xa_ref, xb_ref, w1_ref, b1_ref, w2_ref, b2_ref, ya_ref, yb_ref):
    rows = xa_ref.shape[0]
    row = lax.broadcasted_iota(jnp.int32, (rows, 1), 0)
    h2 = jnp.where(row < n, _unpack(xa_ref[...], xb_ref[...]), 0.0).astype(MXU_DTYPE)
    even = (lax.broadcasted_iota(jnp.int32, (rows, LANES), 1) % 2) == 0
    ff = w2_ref.shape[1]
    z = _dot(h2, w1_ref[0]) + b1_ref[0]
    acts = []
    for m in range(ff // LANES):
        za = z[:, 2 * m * LANES:(2 * m + 1) * LANES]
        zb = z[:, (2 * m + 1) * LANES:(2 * m + 2) * LANES]
        glu = jnp.where(even, za, pltpu.roll(zb, 1, 1))
        lin = jnp.where(even, pltpu.roll(za, LANES - 1, 1), zb)
        glu = jnp.minimum(glu, SWIGLU_LIMIT)
        lin = jnp.clip(lin, -SWIGLU_LIMIT, SWIGLU_LIMIT)
        act = glu * (1.0 / (1.0 + jnp.exp(-SWIGLU_ALPHA * glu))) * (lin + 1.0)
        acts.append(act.astype(MXU_DTYPE))
    y = _dot(jnp.concatenate(acts, axis=1), w2_ref[0]) + b2_ref[0]
    yw = _pack(y.astype(MXU_DTYPE).astype(jnp.float32))
    ya_ref[...] = yw[:, :PACK_W]
    yb_ref[...] = yw[:, PACK_W:]


def _pair_rows(w2):
    ne, ff, d = w2.shape
    half = LANES // 2
    r = np.arange(LANES)
    src = np.where(r % 2 == 0, r // 2, half + r // 2)
    groups = ne * ff // LANES
    perm = jnp.broadcast_to(jnp.asarray(np.eye(LANES, dtype=np.float32)[src], MXU_DTYPE), (groups, LANES, LANES))
    out = jnp.einsum("grk,gkd->grd", perm, w2.astype(MXU_DTYPE).reshape(groups, LANES, d),
                     preferred_element_type=MXU_DTYPE)
    return out.reshape(ne, ff, d)


def _experts(ce, cn, xa, xb, w1, b1, w2, b2):
    nch = ce.shape[0]
    ne, d, f2 = w1.shape
    ff = f2 // 2
    c = MOE_C
    rows = lambda i, ce_, cn_: (i, 0)
    of_expert = lambda i, ce_, cn_: (ce_[i], 0, 0)
    out = jax.ShapeDtypeStruct((nch * c, PACK_W), jnp.int32)
    return pl.pallas_call(
        _expert_kernel,
        grid_spec=pltpu.PrefetchScalarGridSpec(
            num_scalar_prefetch=2,
            grid=(nch,),
            in_specs=[
                pl.BlockSpec((c, PACK_W), rows),
                pl.BlockSpec((c, PACK_W), rows),
                pl.BlockSpec((1, d, f2), of_expert),
                pl.BlockSpec((1, 1, f2), of_expert),
                pl.BlockSpec((1, ff, d), of_expert),
                pl.BlockSpec((1, 1, d), of_expert),
            ],
            out_specs=[pl.BlockSpec((c, PACK_W), rows), pl.BlockSpec((c, PACK_W), rows)],
        ),
        out_shape=[out, out],
        compiler_params=pltpu.CompilerParams(dimension_semantics=("arbitrary",),
                                             vmem_limit_bytes=VMEM_LIMIT),
        name="experts",
    )(ce, cn, xa, xb, w1, b1, w2, b2)


def _combine_kernel(x1_ref, gate_ref, ya_ref, yb_ref, gf_ref, o_ref):
    acc = x1_ref[...]
    gate = gate_ref[...]
    for k in range(TOP_K):
        acc = acc + gate[:, k:k + 1] * _unpack(ya_ref[k], yb_ref[k])
    o_ref[...] = _rms(acc, gf_ref[...])


def _combine(x1, gate, ya, yb, gf):
    t, d = x1.shape
    tm = min(MOE_TM, t)
    row = lambda i: (i, 0)
    picks = lambda i: (0, i, 0)
    return pl.pallas_call(
        _combine_kernel,
        grid=(t // tm,),
        in_specs=[
            pl.BlockSpec((tm, d), row),
            pl.BlockSpec((tm, LANES), row),
            pl.BlockSpec((TOP_K, tm, PACK_W), picks),
            pl.BlockSpec((TOP_K, tm, PACK_W), picks),
            pl.BlockSpec((1, d), lambda i: (0, 0)),
        ],
        out_specs=pl.BlockSpec((tm, d), row),
        out_shape=jax.ShapeDtypeStruct((t, d), jnp.float32),
        compiler_params=pltpu.CompilerParams(dimension_semantics=("parallel",),
                                             vmem_limit_bytes=VMEM_LIMIT),
        name="combine",
    )(x1, gate, ya, yb, gf)


def kernel(x, norm1_gain, w_in, rel_bias, idx_k_ln_gain, idx_k_ln_bias, ret_gn_gain,
           w_out, norm2_gain, w_router, b_router, w_mlp1, b_mlp1, w_mlp2, b_mlp2, final_gain):
    bsz, s, d = x.shape
    depth = w_in.shape[0]
    topk = min(IDX_TOPK_MAX, s // 4)
    assert depth == 1, "the final RMSNorm is fused into the expert kernel of the single layer"
    assert s % DSA_T == 0 and s % RET_C == 0 and topk <= DSA_T and DSA_T % CHUNK == 0
    assert s % min(PROJ_TM, s) == 0 and (bsz * s) % MOE_TM == 0
    ne = w_router.shape[-1]
    assert ne <= LANES and d == 4 * PACK_W
    assert (bsz * s) % SC_WINDOW == 0 and (bsz * s * TOP_K) % MOE_C == 0
    f32 = jnp.float32
    x2 = x.reshape(bsz * s, d).astype(f32)
    cos, sin = _rotary_tables(s)
    bias = _bias_tiles(rel_bias, DSA_T)

    for layer in range(depth):
        wp = _pack_proj_weight(w_in[layer])
        qa, ka, va, qi, ki, wi, qb, kb, vb, sg = _in_proj(
            x2, norm1_gain[layer][None, :].astype(f32), wp, cos, sin,
            idx_k_ln_gain[layer][None, :].astype(f32), idx_k_ln_bias[layer][None, :].astype(f32), s)

        r3 = lambda a: a.reshape(bsz, s, a.shape[-1])
        out_b = _retention(r3(qb), r3(kb), r3(vb), r3(sg),
                           ret_gn_gain[layer][None, :].astype(f32), min(RET_C, s))
        tr = lambda a: jnp.swapaxes(r3(a), 1, 2)
        ones_rows = jnp.zeros((bsz, A_HEADS, V_ROWS - A_HEAD_DIM, s), MXU_DTYPE).at[:, :, 0, :].set(1.0)
        vt = jnp.concatenate([tr(va).reshape(bsz, A_HEADS, A_HEAD_DIM, s), ones_rows],
                             axis=2).reshape(bsz, A_HEADS * V_ROWS, s)
        out_a = _dsa(tr(qa), tr(qi), tr(wi), r3(ka), vt, r3(ki), bias, topk)

        wo = w_out[layer].astype(MXU_DTYPE)
        wr = jnp.pad(w_router[layer], ((0, 0), (0, LANES - ne))).astype(MXU_DTYPE)
        br = jnp.pad(b_router[layer].astype(f32), (0, LANES - ne), constant_values=-jnp.inf)[None, :]
        x1, ha, hb, route, gate, cnt = _out_proj(
            x2, out_a.reshape(bsz * s, A_WIDTH), out_b.reshape(bsz * s, B_WIDTH),
            wo[:A_WIDTH], wo[A_WIDTH:], norm2_gain[layer][None, :].astype(f32), wr, br)

        nch = (bsz * s * TOP_K) // MOE_C + ne
        pos, ce, cn = _route_plan(cnt[:, 0, :], route, ne, nch)
        xa = _sc_scatter_rows(ha, pos, nch * MOE_C)
        xb = _sc_scatter_rows(hb, pos, nch * MOE_C)
        ya, yb = _experts(ce, cn, xa, xb, w_mlp1[layer].astype(MXU_DTYPE), b_mlp1[layer].astype(f32)[:, None, :],
                          _pair_rows(w_mlp2[layer]).astype(MXU_DTYPE), b_mlp2[layer].astype(f32)[:, None, :])
        picks = pos.reshape(-1)
        ga = _sc_gather_rows(ya, picks).reshape(TOP_K, bsz * s, PACK_W)
        gb = _sc_gather_rows(yb, picks).reshape(TOP_K, bsz * s, PACK_W)
        x2 = _combine(x1, gate, ga, gb, final_gain[None, :].astype(f32))

    return x2.reshape(bsz, s, d).astype(x.dtype)
```

```python
import functools
import math

import numpy as np
import jax
import jax.numpy as jnp
from jax import lax
from jax.experimental import pallas as pl
from jax.experimental.pallas import tpu as pltpu
from jax.experimental.pallas import tpu_sc as plsc

CHUNK = 64
A_HEADS = 8
A_HEAD_DIM = 64
A_WIDTH = A_HEADS * A_HEAD_DIM
IDX_HEADS = 8
IDX_DIM = 64
IDX_TOPK_MAX = 256
B_HEADS = 4
B_QK_DIM = 64
B_V_DIM = 128
B_QK_WIDTH = B_HEADS * B_QK_DIM
B_WIDTH = B_HEADS * B_V_DIM
REL_BUCKETS = 32
REL_MAX_DIST = 128
TOP_K = 4
SWIGLU_LIMIT = 7.0
SWIGLU_ALPHA = 1.702
ROPE_BASE = 10000.0
RMS_EPS = 1e-5
LN_EPS = 1e-6

LANES = 128
SUBLANES = 8
VMEM_LIMIT = 56 * 1024 * 1024

MXU_DTYPE = jnp.bfloat16

INT_MIN = -(2 ** 31)
INT_MAX = 2 ** 31 - 1
F32_MIN = float(np.finfo(np.float32).min)
LOG2E = math.log2(math.e)

PROJ_TM = 512
RET_C = 256
DSA_T = 256
COUNT_LANES = 4
V_ROWS = A_HEAD_DIM + 16
MOE_TM = 512
MOE_C = 512
SC_WINDOW = 128
PACK_W = 256


def _dot(a, b):
    return jnp.dot(a, b, preferred_element_type=jnp.float32)


def _dot_nt(a, b):
    return lax.dot_general(a, b, (((1,), (1,)), ((), ())), preferred_element_type=jnp.float32)


def _rms(x, gain):
    return x * lax.rsqrt(jnp.mean(x * x, axis=-1, keepdims=True) + RMS_EPS) * gain


_SEG = {}
_off = 0
for _name, _w in (("qa", A_WIDTH), ("ka", A_WIDTH), ("va", A_WIDTH), ("qi", IDX_HEADS * IDX_DIM),
                  ("ki", LANES), ("wi", LANES),
                  ("qb", B_QK_WIDTH), ("kb", B_QK_WIDTH),
                  ("vb", B_WIDTH), ("gb", B_WIDTH)):
    _SEG[_name] = (_off, _off + _w)
    _off += _w
PROJ_N = _off


def _proj_kernel(x_ref, g_ref, w_ref, cos_ref, sin_ref, lng_ref, lnb_ref,
                 qa_ref, ka_ref, va_ref, qi_ref, ki_ref, wi_ref, qb_ref, kb_ref, vb_ref, sg_ref):
    h = _rms(x_ref[...], g_ref[...]).astype(MXU_DTYPE)

    def seg(name):
        a, b = _SEG[name]
        return _dot(h, w_ref[:, a:b])

    qa_ref[...] = (seg("qa") * (A_HEAD_DIM ** -0.5 * LOG2E)).astype(qa_ref.dtype)
    ka_ref[...] = seg("ka").astype(ka_ref.dtype)
    va_ref[...] = seg("va").astype(va_ref.dtype)
    qi_ref[...] = seg("qi").astype(qi_ref.dtype)
    ki = seg("ki")[:, :IDX_DIM]
    mu = jnp.mean(ki, axis=-1, keepdims=True)
    var = jnp.mean(jnp.square(ki - mu), axis=-1, keepdims=True)
    ki = (ki - mu) * lax.rsqrt(var + LN_EPS) * lng_ref[...] + lnb_ref[...]
    ki_ref[...] = ki.astype(ki_ref.dtype)
    wi_ref[...] = seg("wi")[:, :IDX_HEADS] * ((IDX_HEADS ** -0.5) * (IDX_DIM ** -0.5))
    lane = lax.broadcasted_iota(jnp.int32, (h.shape[0], LANES), 1)
    first = (lane % B_QK_DIM) < B_QK_DIM // 2
    low = lane < B_QK_DIM

    def rotary_slots(name, scale, out_ref):
        x = seg(name)
        for sl in range(B_QK_WIDTH // LANES):
            cols = slice(sl * LANES, (sl + 1) * LANES)
            xs = x[:, cols]
            partner = jnp.where(first, pltpu.roll(xs, LANES - B_QK_DIM // 2, 1), pltpu.roll(xs, B_QK_DIM // 2, 1))
            r = (xs * cos_ref[:, cols] + partner * sin_ref[:, cols]) * scale
            out_ref[:, 2 * sl * LANES:(2 * sl + 1) * LANES] = jnp.where(low, r, 0.0).astype(out_ref.dtype)
            out_ref[:, (2 * sl + 1) * LANES:(2 * sl + 2) * LANES] = jnp.where(
                low, pltpu.roll(r, B_QK_DIM, 1), 0.0).astype(out_ref.dtype)

    rotary_slots("qb", 1.0, qb_ref)
    rotary_slots("kb", B_QK_DIM ** -0.5, kb_ref)
    vb_ref[...] = seg("vb").astype(vb_ref.dtype)
    g = seg("gb")
    sg_ref[...] = g / (1.0 + jnp.exp(-g))


def _pack_proj_weight(w_in):
    sizes = (A_WIDTH, A_WIDTH, A_WIDTH, IDX_HEADS * IDX_DIM, IDX_DIM, IDX_HEADS,
             B_QK_WIDTH, B_QK_WIDTH, B_WIDTH, B_WIDTH)
    cuts = np.cumsum(sizes)[:-1].tolist()
    wqa, wka, wva, wqi, wki, wwi, wqb, wkb, wvb, wgb = jnp.split(w_in, cuts, axis=-1)

    def pad_cols(w, n):
        return jnp.pad(w, ((0, 0), (0, n - w.shape[1])))

    parts = [wqa, wka, wva, wqi, pad_cols(wki, LANES), pad_cols(wwi, LANES), wqb, wkb, wvb, wgb]
    return jnp.concatenate(parts, axis=-1).astype(MXU_DTYPE)


def _rotary_tables(s):
    half = B_QK_DIM // 2
    inv = 1.0 / (ROPE_BASE ** jnp.linspace(0.0, 1.0, half, dtype=jnp.float32))
    ang = jnp.arange(s, dtype=jnp.int32).astype(jnp.float32)[:, None] * inv[None, :]
    cos, sin = jnp.cos(ang), jnp.sin(ang)
    return (jnp.tile(jnp.concatenate([cos, cos], axis=-1), (1, B_HEADS)),
            jnp.tile(jnp.concatenate([-sin, sin], axis=-1), (1, B_HEADS)))


def _in_proj(x2, g1, wp, cos, sin, lng, lnb, s):
    t, d = x2.shape
    tm = min(PROJ_TM, s)
    nt = t // tm
    ns = s // tm
    row = lambda i: (i, 0)
    fixed = lambda i: (0, 0)
    pos = lambda i: (i % ns, 0)
    bw = B_HEADS * LANES
    out_shapes = [
        jax.ShapeDtypeStruct((t, A_WIDTH), MXU_DTYPE),
        jax.ShapeDtypeStruct((t, A_WIDTH), MXU_DTYPE),
        jax.ShapeDtypeStruct((t, A_WIDTH), MXU_DTYPE),
        jax.ShapeDtypeStruct((t, IDX_HEADS * IDX_DIM), MXU_DTYPE),
        jax.ShapeDtypeStruct((t, IDX_DIM), MXU_DTYPE),
        jax.ShapeDtypeStruct((t, IDX_HEADS), jnp.float32),
        jax.ShapeDtypeStruct((t, bw), MXU_DTYPE),
        jax.ShapeDtypeStruct((t, bw), MXU_DTYPE),
        jax.ShapeDtypeStruct((t, B_WIDTH), MXU_DTYPE),
        jax.ShapeDtypeStruct((t, B_WIDTH), jnp.float32),
    ]
    out_specs = [pl.BlockSpec((tm, o.shape[1]), row) for o in out_shapes]
    return pl.pallas_call(
        _proj_kernel,
        grid=(nt,),
        in_specs=[
            pl.BlockSpec((tm, d), row),
            pl.BlockSpec((1, d), fixed),
            pl.BlockSpec((d, PROJ_N), fixed),
            pl.BlockSpec((tm, B_QK_WIDTH), pos),
            pl.BlockSpec((tm, B_QK_WIDTH), pos),
            pl.BlockSpec((1, IDX_DIM), fixed),
            pl.BlockSpec((1, IDX_DIM), fixed),
        ],
        out_specs=out_specs,
        out_shape=out_shapes,
        compiler_params=pltpu.CompilerParams(dimension_semantics=("parallel",),
                                             vmem_limit_bytes=VMEM_LIMIT),
        name="in_proj",
    )(x2, g1, wp, cos, sin, lng, lnb)


def _ret_kernel(q_ref, k_ref, v_ref, sg_ref, gain_ref, decay_ref, zeta_ref, xi_ref, cd_ref,
                o_ref, state_ref):
    @pl.when(pl.program_id(1) == 0)
    def _():
        state_ref[...] = jnp.zeros_like(state_ref)

    for h in range(B_HEADS):
        sl = slice(h * LANES, (h + 1) * LANES)
        q = q_ref[0, :, sl]
        k = k_ref[0, :, sl]
        v = v_ref[0, :, sl]
        state = state_ref[h]
        scores = _dot_nt(q, k) * decay_ref[h]
        intra = _dot(scores.astype(MXU_DTYPE), v)
        cross = _dot(q, state.astype(MXU_DTYPE)) * xi_ref[h]
        y = intra + cross
        kz = (k.astype(jnp.float32) * zeta_ref[h]).T.astype(MXU_DTYPE)
        state_ref[h] = cd_ref[h] * state + _dot(kz, v)
        mu = jnp.mean(y, axis=-1, keepdims=True)
        var = jnp.mean(jnp.square(y - mu), axis=-1, keepdims=True)
        yn = (y - mu) * lax.rsqrt(var + LN_EPS) * gain_ref[:, sl]
        o_ref[0, :, sl] = (sg_ref[0, :, sl] * yn).astype(o_ref.dtype)


def _retention(qb, kb, vb, sg, gain, c):
    b, s, bw = qb.shape
    f32 = jnp.float32
    log_g = jnp.log(1.0 - 2.0 ** (-5.0 - jnp.arange(B_HEADS, dtype=f32)))
    n = jnp.arange(c, dtype=f32)
    diff = n[:, None] - n[None, :]
    decay = jnp.where(diff >= 0, jnp.exp(log_g[:, None, None] * jnp.maximum(diff, 0.0)), 0.0)
    zeta = jnp.exp(log_g[:, None] * (c - 1.0 - n)[None, :])[:, :, None]
    xi = jnp.exp(log_g[:, None] * (n + 1.0)[None, :])[:, :, None]
    cd = jnp.broadcast_to(jnp.exp(log_g * c)[:, None, None], (B_HEADS, 1, LANES))
    blk = lambda bi, ci: (bi, ci, 0)
    fix3 = lambda bi, ci: (0, 0, 0)
    return pl.pallas_call(
        _ret_kernel,
        grid=(b, s // c),
        in_specs=[
            pl.BlockSpec((1, c, bw), blk),
            pl.BlockSpec((1, c, bw), blk),
            pl.BlockSpec((1, c, B_WIDTH), blk),
            pl.BlockSpec((1, c, B_WIDTH), blk),
            pl.BlockSpec((1, B_WIDTH), lambda bi, ci: (0, 0)),
            pl.BlockSpec((B_HEADS, c, c), fix3),
            pl.BlockSpec((B_HEADS, c, 1), fix3),
            pl.BlockSpec((B_HEADS, c, 1), fix3),
            pl.BlockSpec((B_HEADS, 1, LANES), fix3),
        ],
        out_specs=pl.BlockSpec((1, c, B_WIDTH), blk),
        out_shape=jax.ShapeDtypeStruct((b, s, B_WIDTH), MXU_DTYPE),
        scratch_shapes=[pltpu.VMEM((B_HEADS, LANES, LANES), jnp.float32)],
        compiler_params=pltpu.CompilerParams(dimension_semantics=("parallel", "arbitrary"),
                                             vmem_limit_bytes=VMEM_LIMIT),
        name="retention",
    )(qb, kb, vb, sg, gain, decay, zeta, xi, cd)


def _t5_bucket(rel):
    nb = REL_BUCKETS // 2
    max_exact = nb // 2
    ret = jnp.where(rel > 0, nb, 0)
    n = jnp.abs(rel)
    nf = jnp.maximum(n, 1).astype(jnp.float32)
    large = max_exact + (jnp.log(nf / max_exact) / math.log(REL_MAX_DIST / max_exact)
                         * (nb - max_exact)).astype(jnp.int32)
    large = jnp.minimum(large, nb - 1)
    return ret + jnp.where(n < max_exact, n, large)


def _bias_tiles(rel_bias, t):
    i = jnp.arange(t, dtype=jnp.int32)
    rel_d = i[:, None] - i[None, :]
    rb = rel_bias.astype(jnp.float32)
    far = rb[_t5_bucket(jnp.int32(-(t + 1)))]

    def tile(rel):
        hot = jax.nn.one_hot(_t5_bucket(rel), REL_BUCKETS, dtype=jnp.float32)
        return (jnp.einsum("kqb,bh->hkq", hot, rb, precision=lax.Precision.HIGHEST) - far[:, None, None]) * LOG2E

    near = tile(rel_d - t)
    return jnp.stack([jnp.zeros_like(near), near, tile(rel_d)])


def _dsa_kernel(qat_ref, qit_ref, wit_ref, k_ref, vt_ref, ki_ref, bias_ref, o_ref,
                keys_ref, gmax_ref, mb_ref, sbuf_ref, m_ref, al_ref, acc_ref, *, topk):
    t = DSA_T
    i32 = jnp.int32
    qblk = pl.program_id(1)
    ntile = qblk + 1
    qcol = lax.broadcasted_iota(i32, (1, t), 1)
    lim_local = (qcol // CHUNK + 1) * CHUNK
    keep_all = (qblk * t + lim_local) <= topk
    krow = lax.broadcasted_iota(i32, (t, t), 0)

    def tile_off(j):
        return pl.multiple_of(j * t, t)

    qit = [qit_ref[0, h * IDX_DIM:(h + 1) * IDX_DIM, :] for h in range(IDX_HEADS)]
    wit = [wit_ref[0, h:h + 1, :] for h in range(IDX_HEADS)]

    def score_tile(j, diag):
        off = tile_off(j)
        half = t // 2
        for r0 in (0, half):
            start = pl.multiple_of(off + r0, half)
            kt = ki_ref[0, pl.ds(start, half), :]
            sc = jnp.zeros((half, t), jnp.float32)
            for h in range(IDX_HEADS):
                sc = sc + wit[h] * jnp.maximum(_dot(kt, qit[h]), 0.0)
            sc = jnp.where(sc == 0.0, 0.0, sc)
            bits = pltpu.bitcast(sc, i32)
            key = bits ^ ((bits >> 31) & INT_MAX)
            if diag:
                key = jnp.where(lax.broadcasted_iota(i32, (half, t), 0) + r0 < lim_local, key, INT_MIN)
            keys_ref[pl.ds(start, half), :] = key
            gmax_ref[r0:r0 + half, :] = jnp.maximum(gmax_ref[r0:r0 + half, :], key)

    def far_scores(j, c):
        score_tile(j, False)
        return c

    gmax_ref[...] = jnp.full(gmax_ref.shape, INT_MIN, i32)
    lax.fori_loop(0, qblk, far_scores, 0)
    score_tile(qblk, True)

    sub = lax.broadcasted_iota(i32, (SUBLANES, t), 0)

    def rows8(v):
        return jnp.broadcast_to(v, (SUBLANES, t))

    def count(pred):
        def body(j, cnts):
            off = tile_off(j)
            tile = keys_ref[pl.ds(off, t), :]
            cnts = list(cnts)
            for i, r in enumerate(range(0, t, SUBLANES)):
                a = i % len(cnts)
                cnts[a] = cnts[a] + jnp.where(pred(tile[r:r + SUBLANES], sub + (off + r)), 1, 0)
            return tuple(cnts)
        zero = jnp.zeros((SUBLANES, t), i32)
        cnts = lax.fori_loop(0, ntile, body, (zero,) * COUNT_LANES)
        return jnp.sum(functools.reduce(lambda a, b: a + b, cnts), axis=0, keepdims=True)

    def unsettled(lo, hi):
        return jnp.max(jnp.where(hi != lo + 1, 1.0, 0.0))

    def bis_cond(c):
        it, active, _, _ = c
        return jnp.logical_and(it < 32, active > 0.0)

    def bis_pass(lo, hi):
        mid = (lo >> 1) + (hi >> 1) + (lo & hi & 1)
        mid8 = rows8(mid)
        n = count(lambda k, idx: k >= mid8)
        ge = n >= topk
        lo = jnp.where(ge, mid, lo)
        hi = jnp.where(n == topk, mid + 1, jnp.where(ge, hi, mid))
        return lo, hi

    def bis_body(c):
        it, _, lo, hi = c
        lo, hi = bis_pass(*bis_pass(lo, hi))
        return it + 2, unsettled(lo, hi), lo, hi

    gmax = gmax_ref[...]
    kmax = jnp.max(gmax, axis=0, keepdims=True)
    lo0 = jnp.where(keep_all, INT_MIN, jnp.min(gmax, axis=0, keepdims=True))
    hi0 = jnp.where(keep_all, INT_MIN + 1, jnp.where(kmax == INT_MAX, INT_MAX, kmax + 1))
    _, _, lo, _ = lax.while_loop(bis_cond, bis_body, (jnp.int32(0), unsettled(lo0, hi0), lo0, hi0))
    tau = jnp.where(keep_all, INT_MIN + 1, lo)
    tau8 = rows8(tau)
    n_ge = count(lambda k, idx: k >= tau8)
    tie = jnp.logical_and(n_ge > topk, jnp.logical_not(keep_all))

    @pl.when(jnp.max(jnp.where(tie, 1, 0)) > 0)
    def _():
        n_eq_keep = topk - count(lambda k, idx: k > tau8)

        def bis_pos(_, carry):
            plo, phi = carry
            pmid = (plo + phi) >> 1
            pmid8 = rows8(pmid)
            ok = count(lambda k, idx: jnp.logical_and(k == tau8, idx < pmid8)) <= n_eq_keep
            return jnp.where(ok, pmid, plo), jnp.where(ok, phi, pmid)

        smax = pl.num_programs(1) * t
        nbits = int(math.ceil(math.log2(keys_ref.shape[0]))) + 1
        plo, _ = lax.fori_loop(0, nbits, bis_pos,
                               (jnp.zeros((1, t), i32), jnp.zeros((1, t), i32) + smax))
        cut = jnp.where(tie, plo, smax)

        def drop(j, c):
            off = tile_off(j)
            k = keys_ref[pl.ds(off, t), :]
            dead = jnp.logical_and(k == tau, (krow + j * t) >= cut)
            keys_ref[pl.ds(off, t), :] = jnp.where(dead, INT_MIN, k)
            return c

        lax.fori_loop(0, ntile, drop, 0)

    m_ref[...] = jnp.full(m_ref.shape, F32_MIN, jnp.float32)
    al_ref[...] = jnp.ones(al_ref.shape, jnp.float32)
    acc_ref[...] = jnp.zeros(acc_ref.shape, jnp.float32)
    srow = lax.broadcasted_iota(i32, (LANES, t), 0)
    qmt = []
    for h in range(A_HEADS):
        pair = qat_ref[0, (h // 2) * LANES:(h // 2 + 1) * LANES, :]
        qmt.append(jnp.where((srow // A_HEAD_DIM) == (h % 2), pair, jnp.zeros_like(pair)))

    def bias_kind(j):
        return jnp.where(j >= qblk, 2, jnp.where(j >= qblk - 1, 1, 0))

    def set_mask(j):
        mb_ref[...] = jnp.where(keys_ref[pl.ds(tile_off(j), t), :] >= tau, 0.0, -jnp.inf)

    def consume(j, h):
        m_h = m_ref[h]
        a_h = al_ref[h]
        p = jnp.exp2(sbuf_ref[h] - m_h)
        pv = _dot(vt_ref[0, h * V_ROWS:(h + 1) * V_ROWS, pl.ds(tile_off(j), t)], p.astype(MXU_DTYPE))
        acc_ref[h] = a_h * acc_ref[h] + pv

    def produce(j, h, kind):
        g = h // 2
        s = _dot(k_ref[0, pl.ds(tile_off(j), t), g * LANES:(g + 1) * LANES], qmt[h]) + mb_ref[...]
        if kind is not None:
            s = s + bias_ref[kind, h]
        m_h = m_ref[h]
        m_new = jnp.maximum(m_h, jnp.max(s, axis=0, keepdims=True))
        al_ref[h] = jnp.exp2(m_h - m_new)
        m_ref[h] = m_new
        sbuf_ref[h] = s

    set_mask(0)
    for h in range(A_HEADS):
        produce(0, h, bias_kind(0))

    def step(j, kind):
        set_mask(j + 1)
        for h in range(A_HEADS):
            consume(j, h)
            produce(j + 1, h, kind)

    def far_step(j, c):
        step(j, None)
        return c

    def near_step(j, c):
        step(j, bias_kind(j + 1))
        return c

    n_far = jnp.maximum(qblk - 2, 0)
    lax.fori_loop(0, n_far, far_step, 0)
    lax.fori_loop(n_far, qblk, near_step, 0)
    for h in range(A_HEADS):
        consume(qblk, h)

    out_t = jnp.concatenate([acc_ref[h, :A_HEAD_DIM] / acc_ref[h, A_HEAD_DIM:A_HEAD_DIM + 1]
                             for h in range(A_HEADS)], axis=0)
    o_ref[0] = out_t.T.astype(o_ref.dtype)


def _dsa(qat, qit, wit, k, vt, ki, bias, topk):
    b, _, s = qat.shape
    t = DSA_T
    qcols = lambda bi, qi_: (bi, 0, qi_)
    per_b = lambda bi, qi_: (bi, 0, 0)
    once = pl.Buffered(1)
    return pl.pallas_call(
        functools.partial(_dsa_kernel, topk=topk),
        grid=(b, s // t),
        in_specs=[
            pl.BlockSpec((1, A_WIDTH, t), qcols),
            pl.BlockSpec((1, IDX_HEADS * IDX_DIM, t), qcols),
            pl.BlockSpec((1, IDX_HEADS, t), qcols),
            pl.BlockSpec((1, s, A_WIDTH), per_b, pipeline_mode=once),
            pl.BlockSpec((1, A_HEADS * V_ROWS, s), per_b, pipeline_mode=once),
            pl.BlockSpec((1, s, IDX_DIM), per_b, pipeline_mode=once),
            pl.BlockSpec((3, A_HEADS, t, t), lambda bi, qi_: (0, 0, 0, 0), pipeline_mode=once),
        ],
        out_specs=pl.BlockSpec((1, t, A_WIDTH), lambda bi, qi_: (bi, qi_, 0)),
        out_shape=jax.ShapeDtypeStruct((b, s, A_WIDTH), MXU_DTYPE),
        scratch_shapes=[
            pltpu.VMEM((s, t), jnp.int32),
            pltpu.VMEM((t, t), jnp.int32),
            pltpu.VMEM((t, t), jnp.float32),
            pltpu.VMEM((A_HEADS, t, t), jnp.float32),
            pltpu.VMEM((A_HEADS, 1, t), jnp.float32),
            pltpu.VMEM((A_HEADS, 1, t), jnp.float32),
            pltpu.VMEM((A_HEADS, V_ROWS, t), jnp.float32),
        ],
        compiler_params=pltpu.CompilerParams(dimension_semantics=("parallel", "arbitrary"),
                                             vmem_limit_bytes=VMEM_LIMIT),
        name="dsa",
    )(qat, qit, wit, k, vt, ki, bias)


HI16 = -65536


def _pack(v):
    bits = pltpu.bitcast(v, jnp.int32)
    n = v.shape[1] // 2
    return lax.shift_right_logical(bits[:, :n], 16) | (bits[:, n:] & HI16)


def _unpack(wa, wb):
    w = jnp.concatenate([wa, wb], axis=1)
    lo = pltpu.bitcast(w << 16, jnp.float32)
    hi = pltpu.bitcast(w & HI16, jnp.float32)
    return jnp.concatenate([lo, hi], axis=1)


def _outproj_kernel(x_ref, oa_ref, ob_ref, wa_ref, wb_ref, g2_ref, wr_ref, br_ref, tri_ref,
                    x1_ref, ha_ref, hb_ref, route_ref, gate_ref, cnt_ref):
    x1 = x_ref[...] + _dot(oa_ref[...], wa_ref[...]) + _dot(ob_ref[...], wb_ref[...])
    x1_ref[...] = x1
    h2 = _rms(x1, g2_ref[...])
    h2m = h2.astype(MXU_DTYPE)
    hw = _pack(h2m.astype(jnp.float32))
    ha_ref[...] = hw[:, :PACK_W]
    hb_ref[...] = hw[:, PACK_W:]
    logits = _dot(h2m, wr_ref[...]) + br_ref[...]
    lane = lax.broadcasted_iota(jnp.int32, logits.shape, 1)
    work = logits
    vals, hots, ids = [], [], []
    for _ in range(TOP_K):
        mx = jnp.max(work, axis=1, keepdims=True)
        idx = jnp.min(jnp.where(work == mx, lane, LANES), axis=1, keepdims=True)
        hot = lane == idx
        vals.append(mx)
        hots.append(hot)
        ids.append(idx)
        work = jnp.where(hot, -jnp.inf, work)
    es = [jnp.exp(v - vals[0]) for v in vals]
    denom = es[0]
    for e in es[1:]:
        denom = denom + e
    sel = jnp.zeros(logits.shape, jnp.float32)
    for hot in hots:
        sel = sel + jnp.where(hot, 1.0, 0.0)
    rank = _dot(tri_ref[...], sel.astype(MXU_DTYPE))
    route = jnp.zeros(logits.shape, jnp.int32)
    gate = jnp.zeros(logits.shape, jnp.float32)
    for k in range(TOP_K):
        r_k = jnp.sum(jnp.where(hots[k], rank, 0.0), axis=1, keepdims=True).astype(jnp.int32)
        route = route + jnp.where(lane == k, ids[k], 0) + jnp.where(lane == TOP_K + k, r_k, 0)
        gate = gate + jnp.where(lane == k, es[k] / denom, 0.0)
    route_ref[...] = route
    gate_ref[...] = gate
    cnt = jnp.sum(sel, axis=0, keepdims=True).astype(jnp.int32)
    cnt_ref[0] = jnp.broadcast_to(cnt, (SUBLANES, LANES))


def _out_proj(x2, oa, ob, wa, wb, g2, wr, br):
    t, d = x2.shape
    tm = min(PROJ_TM, t)
    row = lambda i: (i, 0)
    fixed = lambda i: (0, 0)
    tri = (jnp.arange(tm)[:, None] > jnp.arange(tm)[None, :]).astype(MXU_DTYPE)
    return pl.pallas_call(
        _outproj_kernel,
        grid=(t // tm,),
        in_specs=[
            pl.BlockSpec((tm, d), row),
            pl.BlockSpec((tm, A_WIDTH), row),
            pl.BlockSpec((tm, B_WIDTH), row),
            pl.BlockSpec((A_WIDTH, d), fixed),
            pl.BlockSpec((B_WIDTH, d), fixed),
            pl.BlockSpec((1, d), fixed),
            pl.BlockSpec((d, LANES), fixed),
            pl.BlockSpec((1, LANES), fixed),
            pl.BlockSpec((tm, tm), fixed),
        ],
        out_specs=[pl.BlockSpec((tm, d), row), pl.BlockSpec((tm, PACK_W), row), pl.BlockSpec((tm, PACK_W), row),
                   pl.BlockSpec((tm, LANES), row), pl.BlockSpec((tm, LANES), row),
                   pl.BlockSpec((1, SUBLANES, LANES), lambda i: (i, 0, 0))],
        out_shape=[jax.ShapeDtypeStruct((t, d), jnp.float32),
                   jax.ShapeDtypeStruct((t, PACK_W), jnp.int32),
                   jax.ShapeDtypeStruct((t, PACK_W), jnp.int32),
                   jax.ShapeDtypeStruct((t, LANES), jnp.int32),
                   jax.ShapeDtypeStruct((t, LANES), jnp.float32),
                   jax.ShapeDtypeStruct((t // tm, SUBLANES, LANES), jnp.int32)],
        compiler_params=pltpu.CompilerParams(dimension_semantics=("parallel",),
                                             vmem_limit_bytes=VMEM_LIMIT),
        name="out_proj",
    )(x2, oa, ob, wa, wb, g2, wr, br, tri)


def _sc_mesh():
    return plsc.VectorSubcoreMesh(core_axis_name="core", subcore_axis_name="subcore")


def _sc_scatter_rows(x, idx, n_rows):
    t, w = x.shape
    nk = idx.shape[0]

    @pl.kernel(out_type=jax.ShapeDtypeStruct((n_rows, w), x.dtype), mesh=_sc_mesh(), scratch_types=[])
    def scatter(x_hbm, i_hbm, o_hbm):
        def body(x_vmem, i_vmem):
            pltpu.sync_copy(x_vmem, o_hbm.at[i_vmem.at[0]])

        pltpu.emit_pipeline(
            body, grid=(t // SC_WINDOW, nk),
            in_specs=[pl.BlockSpec((SC_WINDOW, w), lambda i, k: (i, 0)),
                      pl.BlockSpec((1, SC_WINDOW), lambda i, k: (k, i))],
            out_specs=[],
            core_axis_name=("core", "subcore"),
            dimension_semantics=(pltpu.PARALLEL, pltpu.ARBITRARY),
        )(x_hbm, i_hbm)

    return scatter(x, idx)


def _sc_gather_rows(x, idx):
    n = idx.shape[0]
    w = x.shape[1]

    @pl.kernel(out_type=jax.ShapeDtypeStruct((n, w), x.dtype), mesh=_sc_mesh())
    def gather(x_hbm, i_hbm, o_hbm):
        def body(i_vmem, o_vmem):
            pltpu.sync_copy(x_hbm.at[i_vmem.at[0]], o_vmem)

        pltpu.emit_pipeline(
            body, grid=(n // SC_WINDOW,),
            in_specs=[pl.BlockSpec((1, SC_WINDOW), lambda i: (0, i))],
            out_specs=[pl.BlockSpec((SC_WINDOW, w), lambda i: (i, 0))],
            core_axis_name=("core", "subcore"),
            dimension_semantics=(pltpu.PARALLEL,),
        )(i_hbm, o_hbm)

    return gather(x, idx.reshape(1, n))


def _route_plan(cnt, route, ne, nch):
    c = MOE_C
    tiles = cnt.shape[0]
    tm = route.shape[0] // tiles
    tot = jnp.sum(cnt, axis=0)
    nchunk = (tot + c - 1) // c
    cstart = jnp.cumsum(nchunk) - nchunk
    tile_base = (cstart * c)[None, :] + jnp.cumsum(cnt, axis=0) - cnt
    eid = route[:, :TOP_K]
    rank = route[:, TOP_K:2 * TOP_K]
    base_tok = jnp.repeat(tile_base, tm, axis=0)
    hot = eid[:, :, None] == jnp.arange(LANES, dtype=jnp.int32)[None, None, :]
    pos = jnp.sum(jnp.where(hot, base_tok[:, None, :], 0), axis=-1) + rank
    ci = jnp.arange(nch, dtype=jnp.int32)
    cend = jnp.cumsum(nchunk)[:ne]
    ce = jnp.minimum(jnp.sum(ci[:, None] >= cend[None, :], axis=1), ne - 1).astype(jnp.int32)
    cn = jnp.clip(tot[ce] - (ci - cstart[ce]) * c, 0, c)
    cn = jnp.where(ci < cend[ne - 1], cn, 0).astype(jnp.int32)
    return pos.T, ce, cn


def _expert_kernel(ce_ref, cn_ref, xa_ref, xb_ref, w1f_ref, b1_ref, w2f_ref, b2_ref, perm_ref,
                   ya_ref, yb_ref, w1_ref, w2_ref):
    i = pl.program_id(0)
    n = cn_ref[i]

    @pl.when(n == 0)
    def _():
        ya_ref[...] = jnp.zeros_like(ya_ref)
        yb_ref[...] = jnp.zeros_like(yb_ref)

    @pl.when(jnp.logical_and(n > 0, jnp.logical_or(i == 0, ce_ref[i] != ce_ref[jnp.maximum(i - 1, 0)])))
    def _():
        w1_ref[0] = w1f_ref[0, 0].astype(MXU_DTYPE)
        for m in range(w2_ref.shape[1] // LANES):
            grp = w2f_ref[0, 0, m * LANES:(m + 1) * LANES, :].astype(MXU_DTYPE)
            w2_ref[0, m * LANES:(m + 1) * LANES, :] = _dot(perm_ref[...], grp).astype(MXU_DTYPE)

    @pl.when(n > 0)
    def _():
        _expert_chunk(n, xa_ref, xb_ref, w1_ref, b1_ref, w2_ref, b2_ref, ya_ref, yb_ref)


def _expert_chunk(n, xa_ref, xb_ref, w1_ref, b1_ref, w2_ref, b2_ref, ya_ref, yb_ref):
    rows = xa_ref.shape[0]
    row = lax.broadcasted_iota(jnp.int32, (rows, 1), 0)
    h2 = jnp.where(row < n, _unpack(xa_ref[...], xb_ref[...]), 0.0).astype(MXU_DTYPE)
    even = (lax.broadcasted_iota(jnp.int32, (rows, LANES), 1) % 2) == 0
    ff = w2_ref.shape[1]
    z = _dot(h2, w1_ref[0]) + b1_ref[0]
    acts = []
    for m in range(ff // LANES):
        za = z[:, 2 * m * LANES:(2 * m + 1) * LANES]
        zb = z[:, (2 * m + 1) * LANES:(2 * m + 2) * LANES]
        glu = jnp.where(even, za, pltpu.roll(zb, 1, 1))
        lin = jnp.where(even, pltpu.roll(za, LANES - 1, 1), zb)
        glu = jnp.minimum(glu, SWIGLU_LIMIT)
        lin = jnp.clip(lin, -SWIGLU_LIMIT, SWIGLU_LIMIT)
        act = glu * (1.0 / (1.0 + jnp.exp(-SWIGLU_ALPHA * glu))) * (lin + 1.0)
        acts.append(act.astype(MXU_DTYPE))
    y = _dot(jnp.concatenate(acts, axis=1), w2_ref[0]) + b2_ref[0]
    yw = _pack(y.astype(MXU_DTYPE).astype(jnp.float32))
    ya_ref[...] = yw[:, :PACK_W]
    yb_ref[...] = yw[:, PACK_W:]


def _pair_perm():
    r = np.arange(LANES)
    src = np.where(r % 2 == 0, r // 2, LANES // 2 + r // 2)
    return jnp.asarray(np.eye(LANES, dtype=np.float32)[src], MXU_DTYPE)


def _experts(ce, cn, xa, xb, w1, b1, w2, b2, layer):
    nch = ce.shape[0]
    _, ne, d, f2 = w1.shape
    ff = f2 // 2
    c = MOE_C
    rows = lambda i, ce_, cn_: (i, 0)
    of_expert = lambda i, ce_, cn_: (ce_[i], 0, 0)
    raw = lambda i, ce_, cn_: (layer, ce_[i], 0, 0)
    out = jax.ShapeDtypeStruct((nch * c, PACK_W), jnp.int32)
    return pl.pallas_call(
        _expert_kernel,
        grid_spec=pltpu.PrefetchScalarGridSpec(
            num_scalar_prefetch=2,
            grid=(nch,),
            in_specs=[
                pl.BlockSpec((c, PACK_W), rows),
                pl.BlockSpec((c, PACK_W), rows),
                pl.BlockSpec((1, 1, d, f2), raw),
                pl.BlockSpec((1, 1, f2), of_expert),
                pl.BlockSpec((1, 1, ff, d), raw),
                pl.BlockSpec((1, 1, d), of_expert),
                pl.BlockSpec((LANES, LANES), lambda i, ce_, cn_: (0, 0)),
            ],
            out_specs=[pl.BlockSpec((c, PACK_W), rows), pl.BlockSpec((c, PACK_W), rows)],
            scratch_shapes=[pltpu.VMEM((1, d, f2), MXU_DTYPE),
                            pltpu.VMEM((1, ff, d), MXU_DTYPE)],
        ),
        out_shape=[out, out],
        compiler_params=pltpu.CompilerParams(dimension_semantics=("arbitrary",),
                                             vmem_limit_bytes=VMEM_LIMIT),
        name="experts",
    )(ce, cn, xa, xb, w1, b1, w2, b2, _pair_perm())


def _combine_kernel(x1_ref, gate_ref, ya_ref, yb_ref, gf_ref, o_ref):
    acc = x1_ref[...]
    gate = gate_ref[...]
    for k in range(TOP_K):
        acc = acc + gate[:, k:k + 1] * _unpack(ya_ref[k], yb_ref[k])
    o_ref[...] = _rms(acc, gf_ref[...])


def _combine(x1, gate, ya, yb, gf):
    t, d = x1.shape
    tm = min(MOE_TM, t)
    row = lambda i: (i, 0)
    picks = lambda i: (0, i, 0)
    return pl.pallas_call(
        _combine_kernel,
        grid=(t // tm,),
        in_specs=[
            pl.BlockSpec((tm, d), row),
            pl.BlockSpec((tm, LANES), row),
            pl.BlockSpec((TOP_K, tm, PACK_W), picks),
            pl.BlockSpec((TOP_K, tm, PACK_W), picks),
            pl.BlockSpec((1, d), lambda i: (0, 0)),
        ],
        out_specs=pl.BlockSpec((tm, d), row),
        out_shape=jax.ShapeDtypeStruct((t, d), jnp.float32),
        compiler_params=pltpu.CompilerParams(dimension_semantics=("parallel",),
                                             vmem_limit_bytes=VMEM_LIMIT),
        name="combine",
    )(x1, gate, ya, yb, gf)


def kernel(x, norm1_gain, w_in, rel_bias, idx_k_ln_gain, idx_k_ln_bias, ret_gn_gain,
           w_out, norm2_gain, w_router, b_router, w_mlp1, b_mlp1, w_mlp2, b_mlp2, final_gain):
    bsz, s, d = x.shape
    depth = w_in.shape[0]
    topk = min(IDX_TOPK_MAX, s // 4)
    assert depth == 1, "the final RMSNorm is fused into the expert kernel of the single layer"
    assert s % DSA_T == 0 and s % RET_C == 0 and topk <= DSA_T and DSA_T % CHUNK == 0
    assert s % min(PROJ_TM, s) == 0 and (bsz * s) % MOE_TM == 0
    ne = w_router.shape[-1]
    assert ne <= LANES and d == 4 * PACK_W
    assert (bsz * s) % SC_WINDOW == 0 and (bsz * s * TOP_K) % MOE_C == 0
    f32 = jnp.float32
    x2 = x.reshape(bsz * s, d).astype(f32)
    cos, sin = _rotary_tables(s)
    bias = _bias_tiles(rel_bias, DSA_T)

    for layer in range(depth):
        wp = _pack_proj_weight(w_in[layer])
        qa, ka, va, qi, ki, wi, qb, kb, vb, sg = _in_proj(
            x2, norm1_gain[layer][None, :].astype(f32), wp, cos, sin,
            idx_k_ln_gain[layer][None, :].astype(f32), idx_k_ln_bias[layer][None, :].astype(f32), s)

        r3 = lambda a: a.reshape(bsz, s, a.shape[-1])
        out_b = _retention(r3(qb), r3(kb), r3(vb), r3(sg),
                           ret_gn_gain[layer][None, :].astype(f32), min(RET_C, s))
        tr = lambda a: jnp.swapaxes(r3(a), 1, 2)
        ones_rows = jnp.zeros((bsz, A_HEADS, V_ROWS - A_HEAD_DIM, s), MXU_DTYPE).at[:, :, 0, :].set(1.0)
        vt = jnp.concatenate([tr(va).reshape(bsz, A_HEADS, A_HEAD_DIM, s), ones_rows],
                             axis=2).reshape(bsz, A_HEADS * V_ROWS, s)
        out_a = _dsa(tr(qa), tr(qi), tr(wi), r3(ka), vt, r3(ki), bias, topk)

        wo = w_out[layer].astype(MXU_DTYPE)
        wr = jnp.pad(w_router[layer], ((0, 0), (0, LANES - ne))).astype(MXU_DTYPE)
        br = jnp.pad(b_router[layer].astype(f32), (0, LANES - ne), constant_values=-jnp.inf)[None, :]
        x1, ha, hb, route, gate, cnt = _out_proj(
            x2, out_a.reshape(bsz * s, A_WIDTH), out_b.reshape(bsz * s, B_WIDTH),
            wo[:A_WIDTH], wo[A_WIDTH:], norm2_gain[layer][None, :].astype(f32), wr, br)

        nch = (bsz * s * TOP_K) // MOE_C + ne
        pos, ce, cn = _route_plan(cnt[:, 0, :], route, ne, nch)
        xa = _sc_scatter_rows(ha, pos, nch * MOE_C)
        xb = _sc_scatter_rows(hb, pos, nch * MOE_C)
        ya, yb = _experts(ce, cn, xa, xb, w_mlp1.astype(f32), b_mlp1[layer].astype(f32)[:, None, :],
                          w_mlp2.astype(f32), b_mlp2[layer].astype(f32)[:, None, :], layer)
        picks = pos.reshape(-1)
        ga = _sc_gather_rows(ya, picks).reshape(TOP_K, bsz * s, PACK_W)
        gb = _sc_gather_rows(yb, picks).reshape(TOP_K, bsz * s, PACK_W)
        x2 = _combine(x1, gate, ga, gb, final_gain[None, :].astype(f32))

    return x2.reshape(bsz, s, d).astype(x.dtype)
```

```python
import functools
import math

import numpy as np
import jax
import jax.numpy as jnp
from jax import lax
from jax.experimental import pallas as pl
from jax.experimental.pallas import tpu as pltpu
from jax.experimental.pallas import tpu_sc as plsc

CHUNK = 64
A_HEADS = 8
A_HEAD_DIM = 64
A_WIDTH = A_HEADS * A_HEAD_DIM
IDX_HEADS = 8
IDX_DIM = 64
IDX_TOPK_MAX = 256
B_HEADS = 4
B_QK_DIM = 64
B_V_DIM = 128
B_QK_WIDTH = B_HEADS * B_QK_DIM
B_WIDTH = B_HEADS * B_V_DIM
REL_BUCKETS = 32
REL_MAX_DIST = 128
TOP_K = 4
SWIGLU_LIMIT = 7.0
SWIGLU_ALPHA = 1.702
ROPE_BASE = 10000.0
RMS_EPS = 1e-5
LN_EPS = 1e-6

LANES = 128
SUBLANES = 8
VMEM_LIMIT = 56 * 1024 * 1024

MXU_DTYPE = jnp.bfloat16

INT_MIN = -(2 ** 31)
INT_MAX = 2 ** 31 - 1
F32_MIN = float(np.finfo(np.float32).min)
LOG2E = math.log2(math.e)

PROJ_TM = 512
RET_C = 256
DSA_T = 256
COUNT_LANES = 4
V_ROWS = A_HEAD_DIM + 16
MOE_TM = 512
MOE_C = 1024
SC_WINDOW = 128
PACK_W = 256


def _dot(a, b):
    return jnp.dot(a, b, preferred_element_type=jnp.float32)


def _dot_nt(a, b):
    return lax.dot_general(a, b, (((1,), (1,)), ((), ())), preferred_element_type=jnp.float32)


def _rms(x, gain):
    return x * lax.rsqrt(jnp.mean(x * x, axis=-1, keepdims=True) + RMS_EPS) * gain


_SEG = {}
_off = 0
for _name, _w in (("qa", A_WIDTH), ("ka", A_WIDTH), ("va", A_WIDTH), ("qi", IDX_HEADS * IDX_DIM),
                  ("ki", LANES), ("wi", LANES),
                  ("qb", B_QK_WIDTH), ("kb", B_QK_WIDTH),
                  ("vb", B_WIDTH), ("gb", B_WIDTH)):
    _SEG[_name] = (_off, _off + _w)
    _off += _w
PROJ_N = _off


def _proj_kernel(x_ref, g_ref, w_ref, cos_ref, sin_ref, lng_ref, lnb_ref,
                 qa_ref, ka_ref, va_ref, qi_ref, ki_ref, wi_ref, qb_ref, kb_ref, vb_ref, sg_ref):
    h = _rms(x_ref[...], g_ref[...]).astype(MXU_DTYPE)

    def seg(name):
        a, b = _SEG[name]
        return _dot(h, w_ref[:, a:b])

    qa_ref[...] = (seg("qa") * (A_HEAD_DIM ** -0.5 * LOG2E)).astype(qa_ref.dtype)
    ka_ref[...] = seg("ka").astype(ka_ref.dtype)
    va_ref[...] = seg("va").astype(va_ref.dtype)
    qi_ref[...] = seg("qi").astype(qi_ref.dtype)
    ki = seg("ki")[:, :IDX_DIM]
    mu = jnp.mean(ki, axis=-1, keepdims=True)
    var = jnp.mean(jnp.square(ki - mu), axis=-1, keepdims=True)
    ki = (ki - mu) * lax.rsqrt(var + LN_EPS) * lng_ref[...] + lnb_ref[...]
    ki_ref[...] = ki.astype(ki_ref.dtype)
    wi_ref[...] = seg("wi")[:, :IDX_HEADS] * ((IDX_HEADS ** -0.5) * (IDX_DIM ** -0.5))
    lane = lax.broadcasted_iota(jnp.int32, (h.shape[0], LANES), 1)
    first = (lane % B_QK_DIM) < B_QK_DIM // 2
    low = lane < B_QK_DIM

    def rotary_slots(name, scale, out_ref):
        x = seg(name)
        for sl in range(B_QK_WIDTH // LANES):
            cols = slice(sl * LANES, (sl + 1) * LANES)
            xs = x[:, cols]
            partner = jnp.where(first, pltpu.roll(xs, LANES - B_QK_DIM // 2, 1), pltpu.roll(xs, B_QK_DIM // 2, 1))
            r = (xs * cos_ref[:, cols] + partner * sin_ref[:, cols]) * scale
            out_ref[:, 2 * sl * LANES:(2 * sl + 1) * LANES] = jnp.where(low, r, 0.0).astype(out_ref.dtype)
            out_ref[:, (2 * sl + 1) * LANES:(2 * sl + 2) * LANES] = jnp.where(
                low, pltpu.roll(r, B_QK_DIM, 1), 0.0).astype(out_ref.dtype)

    rotary_slots("qb", 1.0, qb_ref)
    rotary_slots("kb", B_QK_DIM ** -0.5, kb_ref)
    vb_ref[...] = seg("vb").astype(vb_ref.dtype)
    g = seg("gb")
    sg_ref[...] = g / (1.0 + jnp.exp(-g))


def _pack_proj_weight(w_in):
    sizes = (A_WIDTH, A_WIDTH, A_WIDTH, IDX_HEADS * IDX_DIM, IDX_DIM, IDX_HEADS,
             B_QK_WIDTH, B_QK_WIDTH, B_WIDTH, B_WIDTH)
    cuts = np.cumsum(sizes)[:-1].tolist()
    wqa, wka, wva, wqi, wki, wwi, wqb, wkb, wvb, wgb = jnp.split(w_in, cuts, axis=-1)

    def pad_cols(w, n):
        return jnp.pad(w, ((0, 0), (0, n - w.shape[1])))

    parts = [wqa, wka, wva, wqi, pad_cols(wki, LANES), pad_cols(wwi, LANES), wqb, wkb, wvb, wgb]
    return jnp.concatenate(parts, axis=-1).astype(MXU_DTYPE)


def _rotary_tables(s):
    half = B_QK_DIM // 2
    inv = 1.0 / (ROPE_BASE ** jnp.linspace(0.0, 1.0, half, dtype=jnp.float32))
    ang = jnp.arange(s, dtype=jnp.int32).astype(jnp.float32)[:, None] * inv[None, :]
    cos, sin = jnp.cos(ang), jnp.sin(ang)
    return (jnp.tile(jnp.concatenate([cos, cos], axis=-1), (1, B_HEADS)),
            jnp.tile(jnp.concatenate([-sin, sin], axis=-1), (1, B_HEADS)))


def _in_proj(x2, g1, wp, cos, sin, lng, lnb, s):
    t, d = x2.shape
    tm = min(PROJ_TM, s)
    nt = t // tm
    ns = s // tm
    row = lambda i: (i, 0)
    fixed = lambda i: (0, 0)
    pos = lambda i: (i % ns, 0)
    bw = B_HEADS * LANES
    out_shapes = [
        jax.ShapeDtypeStruct((t, A_WIDTH), MXU_DTYPE),
        jax.ShapeDtypeStruct((t, A_WIDTH), MXU_DTYPE),
        jax.ShapeDtypeStruct((t, A_WIDTH), MXU_DTYPE),
        jax.ShapeDtypeStruct((t, IDX_HEADS * IDX_DIM), MXU_DTYPE),
        jax.ShapeDtypeStruct((t, IDX_DIM), MXU_DTYPE),
        jax.ShapeDtypeStruct((t, IDX_HEADS), jnp.float32),
        jax.ShapeDtypeStruct((t, bw), MXU_DTYPE),
        jax.ShapeDtypeStruct((t, bw), MXU_DTYPE),
        jax.ShapeDtypeStruct((t, B_WIDTH), MXU_DTYPE),
        jax.ShapeDtypeStruct((t, B_WIDTH), jnp.float32),
    ]
    out_specs = [pl.BlockSpec((tm, o.shape[1]), row) for o in out_shapes]
    return pl.pallas_call(
        _proj_kernel,
        grid=(nt,),
        in_specs=[
            pl.BlockSpec((tm, d), row),
            pl.BlockSpec((1, d), fixed),
            pl.BlockSpec((d, PROJ_N), fixed),
            pl.BlockSpec((tm, B_QK_WIDTH), pos),
            pl.BlockSpec((tm, B_QK_WIDTH), pos),
            pl.BlockSpec((1, IDX_DIM), fixed),
            pl.BlockSpec((1, IDX_DIM), fixed),
        ],
        out_specs=out_specs,
        out_shape=out_shapes,
        compiler_params=pltpu.CompilerParams(dimension_semantics=("parallel",),
                                             vmem_limit_bytes=VMEM_LIMIT),
        name="in_proj",
    )(x2, g1, wp, cos, sin, lng, lnb)


def _ret_kernel(q_ref, k_ref, v_ref, sg_ref, gain_ref, decay_ref, zeta_ref, xi_ref, cd_ref,
                o_ref, state_ref):
    @pl.when(pl.program_id(1) == 0)
    def _():
        state_ref[...] = jnp.zeros_like(state_ref)

    for h in range(B_HEADS):
        sl = slice(h * LANES, (h + 1) * LANES)
        q = q_ref[0, :, sl]
        k = k_ref[0, :, sl]
        v = v_ref[0, :, sl]
        state = state_ref[h]
        scores = _dot_nt(q, k) * decay_ref[h]
        intra = _dot(scores.astype(MXU_DTYPE), v)
        cross = _dot(q, state.astype(MXU_DTYPE)) * xi_ref[h]
        y = intra + cross
        kz = (k.astype(jnp.float32) * zeta_ref[h]).T.astype(MXU_DTYPE)
        state_ref[h] = cd_ref[h] * state + _dot(kz, v)
        mu = jnp.mean(y, axis=-1, keepdims=True)
        var = jnp.mean(jnp.square(y - mu), axis=-1, keepdims=True)
        yn = (y - mu) * lax.rsqrt(var + LN_EPS) * gain_ref[:, sl]
        o_ref[0, :, sl] = (sg_ref[0, :, sl] * yn).astype(o_ref.dtype)


def _retention(qb, kb, vb, sg, gain, c):
    b, s, bw = qb.shape
    f32 = jnp.float32
    log_g = jnp.log(1.0 - 2.0 ** (-5.0 - jnp.arange(B_HEADS, dtype=f32)))
    n = jnp.arange(c, dtype=f32)
    diff = n[:, None] - n[None, :]
    decay = jnp.where(diff >= 0, jnp.exp(log_g[:, None, None] * jnp.maximum(diff, 0.0)), 0.0)
    zeta = jnp.exp(log_g[:, None] * (c - 1.0 - n)[None, :])[:, :, None]
    xi = jnp.exp(log_g[:, None] * (n + 1.0)[None, :])[:, :, None]
    cd = jnp.broadcast_to(jnp.exp(log_g * c)[:, None, None], (B_HEADS, 1, LANES))
    blk = lambda bi, ci: (bi, ci, 0)
    fix3 = lambda bi, ci: (0, 0, 0)
    return pl.pallas_call(
        _ret_kernel,
        grid=(b, s // c),
        in_specs=[
            pl.BlockSpec((1, c, bw), blk),
            pl.BlockSpec((1, c, bw), blk),
            pl.BlockSpec((1, c, B_WIDTH), blk),
            pl.BlockSpec((1, c, B_WIDTH), blk),
            pl.BlockSpec((1, B_WIDTH), lambda bi, ci: (0, 0)),
            pl.BlockSpec((B_HEADS, c, c), fix3),
            pl.BlockSpec((B_HEADS, c, 1), fix3),
            pl.BlockSpec((B_HEADS, c, 1), fix3),
            pl.BlockSpec((B_HEADS, 1, LANES), fix3),
        ],
        out_specs=pl.BlockSpec((1, c, B_WIDTH), blk),
        out_shape=jax.ShapeDtypeStruct((b, s, B_WIDTH), MXU_DTYPE),
        scratch_shapes=[pltpu.VMEM((B_HEADS, LANES, LANES), jnp.float32)],
        compiler_params=pltpu.CompilerParams(dimension_semantics=("parallel", "arbitrary"),
                                             vmem_limit_bytes=VMEM_LIMIT),
        name="retention",
    )(qb, kb, vb, sg, gain, decay, zeta, xi, cd)


def _t5_bucket(rel):
    nb = REL_BUCKETS // 2
    max_exact = nb // 2
    ret = jnp.where(rel > 0, nb, 0)
    n = jnp.abs(rel)
    nf = jnp.maximum(n, 1).astype(jnp.float32)
    large = max_exact + (jnp.log(nf / max_exact) / math.log(REL_MAX_DIST / max_exact)
                         * (nb - max_exact)).astype(jnp.int32)
    large = jnp.minimum(large, nb - 1)
    return ret + jnp.where(n < max_exact, n, large)


def _bias_tiles(rel_bias, t):
    i = jnp.arange(t, dtype=jnp.int32)
    rel_d = i[:, None] - i[None, :]
    rb = rel_bias.astype(jnp.float32)
    far = rb[_t5_bucket(jnp.int32(-(t + 1)))]

    def tile(rel):
        hot = jax.nn.one_hot(_t5_bucket(rel), REL_BUCKETS, dtype=jnp.float32)
        return (jnp.einsum("kqb,bh->hkq", hot, rb, precision=lax.Precision.HIGHEST) - far[:, None, None]) * LOG2E

    near = tile(rel_d - t)
    return jnp.stack([jnp.zeros_like(near), near, tile(rel_d)])


def _dsa_kernel(qat_ref, qit_ref, wit_ref, k_ref, vt_ref, ki_ref, bias_ref, o_ref,
                keys_ref, gmax_ref, mb_ref, sbuf_ref, m_ref, al_ref, acc_ref, *, topk):
    t = DSA_T
    i32 = jnp.int32
    qblk = pl.program_id(1)
    ntile = qblk + 1
    qcol = lax.broadcasted_iota(i32, (1, t), 1)
    lim_local = (qcol // CHUNK + 1) * CHUNK
    keep_all = (qblk * t + lim_local) <= topk
    krow = lax.broadcasted_iota(i32, (t, t), 0)

    def tile_off(j):
        return pl.multiple_of(j * t, t)

    qit = [qit_ref[0, h * IDX_DIM:(h + 1) * IDX_DIM, :] for h in range(IDX_HEADS)]
    wit = [wit_ref[0, h:h + 1, :] for h in range(IDX_HEADS)]

    def score_tile(j, diag):
        off = tile_off(j)
        half = t // 2
        for r0 in (0, half):
            start = pl.multiple_of(off + r0, half)
            kt = ki_ref[0, pl.ds(start, half), :]
            sc = jnp.zeros((half, t), jnp.float32)
            for h in range(IDX_HEADS):
                sc = sc + wit[h] * jnp.maximum(_dot(kt, qit[h]), 0.0)
            sc = jnp.where(sc == 0.0, 0.0, sc)
            bits = pltpu.bitcast(sc, i32)
            key = bits ^ ((bits >> 31) & INT_MAX)
            if diag:
                key = jnp.where(lax.broadcasted_iota(i32, (half, t), 0) + r0 < lim_local, key, INT_MIN)
            keys_ref[pl.ds(start, half), :] = key
            gmax_ref[r0:r0 + half, :] = jnp.maximum(gmax_ref[r0:r0 + half, :], key)

    def far_scores(j, c):
        score_tile(j, False)
        return c

    gmax_ref[...] = jnp.full(gmax_ref.shape, INT_MIN, i32)
    lax.fori_loop(0, qblk, far_scores, 0)
    score_tile(qblk, True)

    sub = lax.broadcasted_iota(i32, (SUBLANES, t), 0)

    def rows8(v):
        return jnp.broadcast_to(v, (SUBLANES, t))

    def count(pred):
        def body(j, cnts):
            off = tile_off(j)
            tile = keys_ref[pl.ds(off, t), :]
            cnts = list(cnts)
            for i, r in enumerate(range(0, t, SUBLANES)):
                a = i % len(cnts)
                cnts[a] = cnts[a] + jnp.where(pred(tile[r:r + SUBLANES], sub + (off + r)), 1, 0)
            return tuple(cnts)
        zero = jnp.zeros((SUBLANES, t), i32)
        cnts = lax.fori_loop(0, ntile, body, (zero,) * COUNT_LANES)
        return jnp.sum(functools.reduce(lambda a, b: a + b, cnts), axis=0, keepdims=True)

    def unsettled(lo, hi):
        return jnp.max(jnp.where(hi != lo + 1, 1.0, 0.0))

    def bis_cond(c):
        it, active, _, _, _ = c
        return jnp.logical_and(it < 32, active > 0.0)

    def bis_pass(lo, hi, nlo):
        mid = (lo >> 1) + (hi >> 1) + (lo & hi & 1)
        mid8 = rows8(mid)
        n = count(lambda k, idx: k >= mid8)
        ge = n >= topk
        lo = jnp.where(ge, mid, lo)
        nlo = jnp.where(ge, n, nlo)
        hi = jnp.where(n == topk, mid + 1, jnp.where(ge, hi, mid))
        return lo, hi, nlo

    def bis_body(c):
        it, _, lo, hi, nlo = c
        lo, hi, nlo = bis_pass(*bis_pass(lo, hi, nlo))
        return it + 2, unsettled(lo, hi), lo, hi, nlo

    gmax = gmax_ref[...]
    kmax = jnp.max(gmax, axis=0, keepdims=True)
    lo0 = jnp.where(keep_all, INT_MIN, jnp.min(gmax, axis=0, keepdims=True))
    hi0 = jnp.where(keep_all, INT_MIN + 1, jnp.where(kmax == INT_MAX, INT_MAX, kmax + 1))
    unknown = jnp.full((1, t), -1, i32)
    _, _, lo, _, nlo = lax.while_loop(bis_cond, bis_body,
                                      (jnp.int32(0), unsettled(lo0, hi0), lo0, hi0, unknown))
    tau = jnp.where(keep_all, INT_MIN + 1, lo)
    tau8 = rows8(tau)
    recount = jnp.max(jnp.where(jnp.logical_and(nlo < 0, jnp.logical_not(keep_all)), 1.0, 0.0)) > 0.0
    n_ge = lax.cond(recount, lambda: count(lambda k, idx: k >= tau8), lambda: nlo)
    tie = jnp.logical_and(n_ge > topk, jnp.logical_not(keep_all))

    @pl.when(jnp.max(jnp.where(tie, 1, 0)) > 0)
    def _():
        n_eq_keep = topk - count(lambda k, idx: k > tau8)

        def bis_pos(_, carry):
            plo, phi = carry
            pmid = (plo + phi) >> 1
            pmid8 = rows8(pmid)
            ok = count(lambda k, idx: jnp.logical_and(k == tau8, idx < pmid8)) <= n_eq_keep
            return jnp.where(ok, pmid, plo), jnp.where(ok, phi, pmid)

        smax = pl.num_programs(1) * t
        nbits = int(math.ceil(math.log2(keys_ref.shape[0]))) + 1
        plo, _ = lax.fori_loop(0, nbits, bis_pos,
                               (jnp.zeros((1, t), i32), jnp.zeros((1, t), i32) + smax))
        cut = jnp.where(tie, plo, smax)

        def drop(j, c):
            off = tile_off(j)
            k = keys_ref[pl.ds(off, t), :]
            dead = jnp.logical_and(k == tau, (krow + j * t) >= cut)
            keys_ref[pl.ds(off, t), :] = jnp.where(dead, INT_MIN, k)
            return c

        lax.fori_loop(0, ntile, drop, 0)

    m_ref[...] = jnp.full(m_ref.shape, F32_MIN, jnp.float32)
    al_ref[...] = jnp.ones(al_ref.shape, jnp.float32)
    acc_ref[...] = jnp.zeros(acc_ref.shape, jnp.float32)
    srow = lax.broadcasted_iota(i32, (LANES, t), 0)
    qmt = []
    for h in range(A_HEADS):
        pair = qat_ref[0, (h // 2) * LANES:(h // 2 + 1) * LANES, :]
        qmt.append(jnp.where((srow // A_HEAD_DIM) == (h % 2), pair, jnp.zeros_like(pair)))

    def bias_kind(j):
        return jnp.where(j >= qblk, 2, jnp.where(j >= qblk - 1, 1, 0))

    def set_mask(j):
        mb_ref[...] = jnp.where(keys_ref[pl.ds(tile_off(j), t), :] >= tau, 0.0, -jnp.inf)

    def consume(j, h):
        m_h = m_ref[h]
        a_h = al_ref[h]
        p = jnp.exp2(sbuf_ref[h] - m_h)
        pv = _dot(vt_ref[0, h * V_ROWS:(h + 1) * V_ROWS, pl.ds(tile_off(j), t)], p.astype(MXU_DTYPE))
        acc_ref[h] = a_h * acc_ref[h] + pv

    def produce(j, h, kind):
        g = h // 2
        s = _dot(k_ref[0, pl.ds(tile_off(j), t), g * LANES:(g + 1) * LANES], qmt[h]) + mb_ref[...]
        if kind is not None:
            s = s + bias_ref[kind, h]
        m_h = m_ref[h]
        m_new = jnp.maximum(m_h, jnp.max(s, axis=0, keepdims=True))
        al_ref[h] = jnp.exp2(m_h - m_new)
        m_ref[h] = m_new
        sbuf_ref[h] = s

    set_mask(0)
    for h in range(A_HEADS):
        produce(0, h, bias_kind(0))

    def step(j, kind):
        set_mask(j + 1)
        for h in range(A_HEADS):
            consume(j, h)
            produce(j + 1, h, kind)

    def far_step(j, c):
        step(j, None)
        return c

    def near_step(j, c):
        step(j, bias_kind(j + 1))
        return c

    n_far = jnp.maximum(qblk - 2, 0)
    lax.fori_loop(0, n_far, far_step, 0)
    lax.fori_loop(n_far, qblk, near_step, 0)
    for h in range(A_HEADS):
        consume(qblk, h)

    out_t = jnp.concatenate([acc_ref[h, :A_HEAD_DIM] / acc_ref[h, A_HEAD_DIM:A_HEAD_DIM + 1]
                             for h in range(A_HEADS)], axis=0)
    o_ref[0] = out_t.T.astype(o_ref.dtype)


def _dsa(qat, qit, wit, k, vt, ki, bias, topk):
    b, _, s = qat.shape
    t = DSA_T
    qcols = lambda bi, qi_: (bi, 0, qi_)
    per_b = lambda bi, qi_: (bi, 0, 0)
    once = pl.Buffered(1)
    return pl.pallas_call(
        functools.partial(_dsa_kernel, topk=topk),
        grid=(b, s // t),
        in_specs=[
            pl.BlockSpec((1, A_WIDTH, t), qcols),
            pl.BlockSpec((1, IDX_HEADS * IDX_DIM, t), qcols),
            pl.BlockSpec((1, IDX_HEADS, t), qcols),
            pl.BlockSpec((1, s, A_WIDTH), per_b, pipeline_mode=once),
            pl.BlockSpec((1, A_HEADS * V_ROWS, s), per_b, pipeline_mode=once),
            pl.BlockSpec((1, s, IDX_DIM), per_b, pipeline_mode=once),
            pl.BlockSpec((3, A_HEADS, t, t), lambda bi, qi_: (0, 0, 0, 0), pipeline_mode=once),
        ],
        out_specs=pl.BlockSpec((1, t, A_WIDTH), lambda bi, qi_: (bi, qi_, 0)),
        out_shape=jax.ShapeDtypeStruct((b, s, A_WIDTH), MXU_DTYPE),
        scratch_shapes=[
            pltpu.VMEM((s, t), jnp.int32),
            pltpu.VMEM((t, t), jnp.int32),
            pltpu.VMEM((t, t), jnp.float32),
            pltpu.VMEM((A_HEADS, t, t), jnp.float32),
            pltpu.VMEM((A_HEADS, 1, t), jnp.float32),
            pltpu.VMEM((A_HEADS, 1, t), jnp.float32),
            pltpu.VMEM((A_HEADS, V_ROWS, t), jnp.float32),
        ],
        compiler_params=pltpu.CompilerParams(dimension_semantics=("parallel", "arbitrary"),
                                             vmem_limit_bytes=VMEM_LIMIT),
        name="dsa",
    )(qat, qit, wit, k, vt, ki, bias)


HI16 = -65536


def _pack(v):
    bits = pltpu.bitcast(v, jnp.int32)
    n = v.shape[1] // 2
    return lax.shift_right_logical(bits[:, :n], 16) | (bits[:, n:] & HI16)


def _unpack(wa, wb):
    w = jnp.concatenate([wa, wb], axis=1)
    lo = pltpu.bitcast(w << 16, jnp.float32)
    hi = pltpu.bitcast(w & HI16, jnp.float32)
    return jnp.concatenate([lo, hi], axis=1)


def _outproj_kernel(x_ref, oa_ref, ob_ref, wa_ref, wb_ref, g2_ref, wr_ref, br_ref, tri_ref,
                    x1_ref, ha_ref, hb_ref, route_ref, gate_ref, cnt_ref):
    x1 = x_ref[...] + _dot(oa_ref[...], wa_ref[...]) + _dot(ob_ref[...], wb_ref[...])
    x1_ref[...] = x1
    h2 = _rms(x1, g2_ref[...])
    h2m = h2.astype(MXU_DTYPE)
    hw = _pack(h2m.astype(jnp.float32))
    ha_ref[...] = hw[:, :PACK_W]
    hb_ref[...] = hw[:, PACK_W:]
    logits = _dot(h2m, wr_ref[...]) + br_ref[...]
    lane = lax.broadcasted_iota(jnp.int32, logits.shape, 1)
    work = logits
    vals, hots, ids = [], [], []
    for _ in range(TOP_K):
        mx = jnp.max(work, axis=1, keepdims=True)
        idx = jnp.min(jnp.where(work == mx, lane, LANES), axis=1, keepdims=True)
        hot = lane == idx
        vals.append(mx)
        hots.append(hot)
        ids.append(idx)
        work = jnp.where(hot, -jnp.inf, work)
    es = [jnp.exp(v - vals[0]) for v in vals]
    denom = es[0]
    for e in es[1:]:
        denom = denom + e
    sel = jnp.zeros(logits.shape, jnp.float32)
    for hot in hots:
        sel = sel + jnp.where(hot, 1.0, 0.0)
    rank = _dot(tri_ref[...], sel.astype(MXU_DTYPE))
    route = jnp.zeros(logits.shape, jnp.int32)
    gate = jnp.zeros(logits.shape, jnp.float32)
    for k in range(TOP_K):
        r_k = jnp.sum(jnp.where(hots[k], rank, 0.0), axis=1, keepdims=True).astype(jnp.int32)
        route = route + jnp.where(lane == k, ids[k], 0) + jnp.where(lane == TOP_K + k, r_k, 0)
        gate = gate + jnp.where(lane == k, es[k] / denom, 0.0)
    route_ref[...] = route
    gate_ref[...] = gate
    cnt = jnp.sum(sel, axis=0, keepdims=True).astype(jnp.int32)
    cnt_ref[0] = jnp.broadcast_to(cnt, (SUBLANES, LANES))


def _out_proj(x2, oa, ob, wa, wb, g2, wr, br):
    t, d = x2.shape
    tm = min(PROJ_TM, t)
    row = lambda i: (i, 0)
    fixed = lambda i: (0, 0)
    tri = (jnp.arange(tm)[:, None] > jnp.arange(tm)[None, :]).astype(MXU_DTYPE)
    return pl.pallas_call(
        _outproj_kernel,
        grid=(t // tm,),
        in_specs=[
            pl.BlockSpec((tm, d), row),
            pl.BlockSpec((tm, A_WIDTH), row),
            pl.BlockSpec((tm, B_WIDTH), row),
            pl.BlockSpec((A_WIDTH, d), fixed),
            pl.BlockSpec((B_WIDTH, d), fixed),
            pl.BlockSpec((1, d), fixed),
            pl.BlockSpec((d, LANES), fixed),
            pl.BlockSpec((1, LANES), fixed),
            pl.BlockSpec((tm, tm), fixed),
        ],
        out_specs=[pl.BlockSpec((tm, d), row), pl.BlockSpec((tm, PACK_W), row), pl.BlockSpec((tm, PACK_W), row),
                   pl.BlockSpec((tm, LANES), row), pl.BlockSpec((tm, LANES), row),
                   pl.BlockSpec((1, SUBLANES, LANES), lambda i: (i, 0, 0))],
        out_shape=[jax.ShapeDtypeStruct((t, d), jnp.float32),
                   jax.ShapeDtypeStruct((t, PACK_W), jnp.int32),
                   jax.ShapeDtypeStruct((t, PACK_W), jnp.int32),
                   jax.ShapeDtypeStruct((t, LANES), jnp.int32),
                   jax.ShapeDtypeStruct((t, LANES), jnp.float32),
                   jax.ShapeDtypeStruct((t // tm, SUBLANES, LANES), jnp.int32)],
        compiler_params=pltpu.CompilerParams(dimension_semantics=("parallel",),
                                             vmem_limit_bytes=VMEM_LIMIT),
        name="out_proj",
    )(x2, oa, ob, wa, wb, g2, wr, br, tri)


def _sc_mesh():
    return plsc.VectorSubcoreMesh(core_axis_name="core", subcore_axis_name="subcore")


def _sc_scatter_rows(x, idx, n_rows):
    t, w = x.shape
    nk = idx.shape[0]

    @pl.kernel(out_type=jax.ShapeDtypeStruct((n_rows, w), x.dtype), mesh=_sc_mesh(), scratch_types=[])
    def scatter(x_hbm, i_hbm, o_hbm):
        def body(x_vmem, i_vmem):
            pltpu.sync_copy(x_vmem, o_hbm.at[i_vmem.at[0]])

        pltpu.emit_pipeline(
            body, grid=(t // SC_WINDOW, nk),
            in_specs=[pl.BlockSpec((SC_WINDOW, w), lambda i, k: (i, 0)),
                      pl.BlockSpec((1, SC_WINDOW), lambda i, k: (k, i))],
            out_specs=[],
            core_axis_name=("core", "subcore"),
            dimension_semantics=(pltpu.PARALLEL, pltpu.ARBITRARY),
        )(x_hbm, i_hbm)

    return scatter(x, idx)


def _sc_gather_rows(x, idx):
    n = idx.shape[0]
    w = x.shape[1]

    @pl.kernel(out_type=jax.ShapeDtypeStruct((n, w), x.dtype), mesh=_sc_mesh())
    def gather(x_hbm, i_hbm, o_hbm):
        def body(i_vmem, o_vmem):
            pltpu.sync_copy(x_hbm.at[i_vmem.at[0]], o_vmem)

        pltpu.emit_pipeline(
            body, grid=(n // SC_WINDOW,),
            in_specs=[pl.BlockSpec((1, SC_WINDOW), lambda i: (0, i))],
            out_specs=[pl.BlockSpec((SC_WINDOW, w), lambda i: (i, 0))],
            core_axis_name=("core", "subcore"),
            dimension_semantics=(pltpu.PARALLEL,),
        )(i_hbm, o_hbm)

    return gather(x, idx.reshape(1, n))


def _route_plan(cnt, route, ne, nch):
    c = MOE_C
    tiles = cnt.shape[0]
    tm = route.shape[0] // tiles
    tot = jnp.sum(cnt, axis=0)
    nchunk = (tot + c - 1) // c
    cstart = jnp.cumsum(nchunk) - nchunk
    tile_base = (cstart * c)[None, :] + jnp.cumsum(cnt, axis=0) - cnt
    eid = route[:, :TOP_K]
    rank = route[:, TOP_K:2 * TOP_K]
    base_tok = jnp.repeat(tile_base, tm, axis=0)
    hot = eid[:, :, None] == jnp.arange(LANES, dtype=jnp.int32)[None, None, :]
    pos = jnp.sum(jnp.where(hot, base_tok[:, None, :], 0), axis=-1) + rank
    ci = jnp.arange(nch, dtype=jnp.int32)
    cend = jnp.cumsum(nchunk)[:ne]
    ce = jnp.minimum(jnp.sum(ci[:, None] >= cend[None, :], axis=1), ne - 1).astype(jnp.int32)
    cn = jnp.clip(tot[ce] - (ci - cstart[ce]) * c, 0, c)
    cn = jnp.where(ci < cend[ne - 1], cn, 0).astype(jnp.int32)
    return pos.T, ce, cn


def _expert_kernel(ce_ref, cn_ref, xa_ref, xb_ref, w1f_ref, b1_ref, w2f_ref, b2_ref, perm_ref,
                   ya_ref, yb_ref, w1_ref, w2_ref):
    i = pl.program_id(0)
    n = cn_ref[i]

    @pl.when(n == 0)
    def _():
        ya_ref[...] = jnp.zeros_like(ya_ref)
        yb_ref[...] = jnp.zeros_like(yb_ref)

    @pl.when(jnp.logical_and(n > 0, jnp.logical_or(i == 0, ce_ref[i] != ce_ref[jnp.maximum(i - 1, 0)])))
    def _():
        w1_ref[0] = w1f_ref[0, 0].astype(MXU_DTYPE)
        for m in range(w2_ref.shape[1] // LANES):
            grp = w2f_ref[0, 0, m * LANES:(m + 1) * LANES, :].astype(MXU_DTYPE)
            w2_ref[0, m * LANES:(m + 1) * LANES, :] = _dot(perm_ref[...], grp).astype(MXU_DTYPE)

    @pl.when(n > 0)
    def _():
        _expert_chunk(n, xa_ref, xb_ref, w1_ref, b1_ref, w2_ref, b2_ref, ya_ref, yb_ref)


def _expert_chunk(n, xa_ref, xb_ref, w1_ref, b1_ref, w2_ref, b2_ref, ya_ref, yb_ref):
    rows = xa_ref.shape[0]
    row = lax.broadcasted_iota(jnp.int32, (rows, 1), 0)
    h2 = jnp.where(row < n, _unpack(xa_ref[...], xb_ref[...]), 0.0).astype(MXU_DTYPE)
    even = (lax.broadcasted_iota(jnp.int32, (rows, LANES), 1) % 2) == 0
    ff = w2_ref.shape[1]
    z = _dot(h2, w1_ref[0]) + b1_ref[0]
    acts = []
    for m in range(ff // LANES):
        za = z[:, 2 * m * LANES:(2 * m + 1) * LANES]
        zb = z[:, (2 * m + 1) * LANES:(2 * m + 2) * LANES]
        glu = jnp.where(even, za, pltpu.roll(zb, 1, 1))
        lin = jnp.where(even, pltpu.roll(za, LANES - 1, 1), zb)
        glu = jnp.minimum(glu, SWIGLU_LIMIT)
        lin = jnp.clip(lin, -SWIGLU_LIMIT, SWIGLU_LIMIT)
        act = glu * (1.0 / (1.0 + jnp.exp(-SWIGLU_ALPHA * glu))) * (lin + 1.0)
        acts.append(act.astype(MXU_DTYPE))
    y = _dot(jnp.concatenate(acts, axis=1), w2_ref[0]) + b2_ref[0]
    yw = _pack(y.astype(MXU_DTYPE).astype(jnp.float32))
    ya_ref[...] = yw[:, :PACK_W]
    yb_ref[...] = yw[:, PACK_W:]


def _pair_perm():
    r = np.arange(LANES)
    src = np.where(r % 2 == 0, r // 2, LANES // 2 + r // 2)
    return jnp.asarray(np.eye(LANES, dtype=np.float32)[src], MXU_DTYPE)


def _experts(ce, cn, xa, xb, w1, b1, w2, b2, layer):
    nch = ce.shape[0]
    _, ne, d, f2 = w1.shape
    ff = f2 // 2
    c = MOE_C
    rows = lambda i, ce_, cn_: (i, 0)
    of_expert = lambda i, ce_, cn_: (ce_[i], 0, 0)
    raw = lambda i, ce_, cn_: (layer, ce_[i], 0, 0)
    out = jax.ShapeDtypeStruct((nch * c, PACK_W), jnp.int32)
    return pl.pallas_call(
        _expert_kernel,
        grid_spec=pltpu.PrefetchScalarGridSpec(
            num_scalar_prefetch=2,
            grid=(nch,),
            in_specs=[
                pl.BlockSpec((c, PACK_W), rows),
                pl.BlockSpec((c, PACK_W), rows),
                pl.BlockSpec((1, 1, d, f2), raw),
                pl.BlockSpec((1, 1, f2), of_expert),
                pl.BlockSpec((1, 1, ff, d), raw),
                pl.BlockSpec((1, 1, d), of_expert),
                pl.BlockSpec((LANES, LANES), lambda i, ce_, cn_: (0, 0)),
            ],
            out_specs=[pl.BlockSpec((c, PACK_W), rows), pl.BlockSpec((c, PACK_W), rows)],
            scratch_shapes=[pltpu.VMEM((1, d, f2), MXU_DTYPE),
                            pltpu.VMEM((1, ff, d), MXU_DTYPE)],
        ),
        out_shape=[out, out],
        compiler_params=pltpu.CompilerParams(dimension_semantics=("arbitrary",),
                                             vmem_limit_bytes=VMEM_LIMIT),
        name="experts",
    )(ce, cn, xa, xb, w1, b1, w2, b2, _pair_perm())


def _combine_kernel(x1_ref, gate_ref, ya_ref, yb_ref, gf_ref, o_ref):
    acc = x1_ref[...]
    gate = gate_ref[...]
    for k in range(TOP_K):
        acc = acc + gate[:, k:k + 1] * _unpack(ya_ref[k], yb_ref[k])
    o_ref[...] = _rms(acc, gf_ref[...])


def _combine(x1, gate, ya, yb, gf):
    t, d = x1.shape
    tm = min(MOE_TM, t)
    row = lambda i: (i, 0)
    picks = lambda i: (0, i, 0)
    return pl.pallas_call(
        _combine_kernel,
        grid=(t // tm,),
        in_specs=[
            pl.BlockSpec((tm, d), row),
            pl.BlockSpec((tm, LANES), row),
            pl.BlockSpec((TOP_K, tm, PACK_W), picks),
            pl.BlockSpec((TOP_K, tm, PACK_W), picks),
            pl.BlockSpec((1, d), lambda i: (0, 0)),
        ],
        out_specs=pl.BlockSpec((tm, d), row),
        out_shape=jax.ShapeDtypeStruct((t, d), jnp.float32),
        compiler_params=pltpu.CompilerParams(dimension_semantics=("parallel",),
                                             vmem_limit_bytes=VMEM_LIMIT),
        name="combine",
    )(x1, gate, ya, yb, gf)


def kernel(x, norm1_gain, w_in, rel_bias, idx_k_ln_gain, idx_k_ln_bias, ret_gn_gain,
           w_out, norm2_gain, w_router, b_router, w_mlp1, b_mlp1, w_mlp2, b_mlp2, final_gain):
    bsz, s, d = x.shape
    depth = w_in.shape[0]
    topk = min(IDX_TOPK_MAX, s // 4)
    assert depth == 1, "the final RMSNorm is fused into the expert kernel of the single layer"
    assert s % DSA_T == 0 and s % RET_C == 0 and topk <= DSA_T and DSA_T % CHUNK == 0
    assert s % min(PROJ_TM, s) == 0 and (bsz * s) % MOE_TM == 0
    ne = w_router.shape[-1]
    assert ne <= LANES and d == 4 * PACK_W
    assert (bsz * s) % SC_WINDOW == 0 and (bsz * s * TOP_K) % MOE_C == 0
    f32 = jnp.float32
    x2 = x.reshape(bsz * s, d).astype(f32)
    cos, sin = _rotary_tables(s)
    bias = _bias_tiles(rel_bias, DSA_T)

    for layer in range(depth):
        wp = _pack_proj_weight(w_in[layer])
        qa, ka, va, qi, ki, wi, qb, kb, vb, sg = _in_proj(
            x2, norm1_gain[layer][None, :].astype(f32), wp, cos, sin,
            idx_k_ln_gain[layer][None, :].astype(f32), idx_k_ln_bias[layer][None, :].astype(f32), s)

        r3 = lambda a: a.reshape(bsz, s, a.shape[-1])
        out_b = _retention(r3(qb), r3(kb), r3(vb), r3(sg),
                           ret_gn_gain[layer][None, :].astype(f32), min(RET_C, s))
        tr = lambda a: jnp.swapaxes(r3(a), 1, 2)
        ones_rows = jnp.zeros((bsz, A_HEADS, V_ROWS - A_HEAD_DIM, s), MXU_DTYPE).at[:, :, 0, :].set(1.0)
        vt = jnp.concatenate([tr(va).reshape(bsz, A_HEADS, A_HEAD_DIM, s), ones_rows],
                             axis=2).reshape(bsz, A_HEADS * V_ROWS, s)
        out_a = _dsa(tr(qa), tr(qi), tr(wi), r3(ka), vt, r3(ki), bias, topk)

        wo = w_out[layer].astype(MXU_DTYPE)
        wr = jnp.pad(w_router[layer], ((0, 0), (0, LANES - ne))).astype(MXU_DTYPE)
        br = jnp.pad(b_router[layer].astype(f32), (0, LANES - ne), constant_values=-jnp.inf)[None, :]
        x1, ha, hb, route, gate, cnt = _out_proj(
            x2, out_a.reshape(bsz * s, A_WIDTH), out_b.reshape(bsz * s, B_WIDTH),
            wo[:A_WIDTH], wo[A_WIDTH:], norm2_gain[layer][None, :].astype(f32), wr, br)

        nch = (bsz * s * TOP_K) // MOE_C + ne
        pos, ce, cn = _route_plan(cnt[:, 0, :], route, ne, nch)
        xa = _sc_scatter_rows(ha, pos, nch * MOE_C)
        xb = _sc_scatter_rows(hb, pos, nch * MOE_C)
        ya, yb = _experts(ce, cn, xa, xb, w_mlp1.astype(f32), b_mlp1[layer].astype(f32)[:, None, :],
                          w_mlp2.astype(f32), b_mlp2[layer].astype(f32)[:, None, :], layer)
        picks = pos.reshape(-1)
        ga = _sc_gather_rows(ya, picks).reshape(TOP_K, bsz * s, PACK_W)
        gb = _sc_gather_rows(yb, picks).reshape(TOP_K, bsz * s, PACK_W)
        x2 = _combine(x1, gate, ga, gb, final_gain[None, :].astype(f32))

    return x2.reshape(bsz, s, d).astype(x.dtype)
```
